```python
import math
import jax, jax.numpy as jnp
from jax import lax
import numpy as np

D_MODEL = 1024
BATCH = 2
SEQ = 8192
DEPTH = 1

ROPE_THETA = 500000.0
NORM_EPS = 1e-6
NEG_INF = -1e30
BLK = 128

A_HEADS = 8
HEAD_DIM = 64
A_ROT_DIM = HEAD_DIM // 4
DIL_PATTERNS = ((128, 1), (512, 4), (2048, 16))
A_WIDTH = A_HEADS * HEAD_DIM

B_HEADS = 8
MLA_Q_RANK = 256
MLA_KV_RANK = 128
MLA_NOPE = 64
MLA_ROPE = 32
MLA_V = 64
B_WIDTH = B_HEADS * MLA_V

MIX_WIDTH = A_WIDTH + B_WIDTH
IN_SPLITS = (A_WIDTH, A_WIDTH, A_WIDTH, MLA_Q_RANK, MLA_KV_RANK, MLA_ROPE)
IN_WIDTH = sum(IN_SPLITS)

PEER_HEADS = 8
PEER_NKEYS = 128
PEER_EXPERTS = PEER_NKEYS * PEER_NKEYS
PEER_QDIM = 256
PEER_TOPK = 16
PEER_CHUNK = 128

kernel_name = "hybrid_dilated_mla_peer_layer"


def rmsnorm(x, g):
    xf = x.astype(jnp.float32)
    y = xf * lax.rsqrt(jnp.mean(xf * xf, axis=-1, keepdims=True) + NORM_EPS)
    return (y * g.astype(jnp.float32)).astype(x.dtype)


def rope(x, pos, rot_dim):
    half = rot_dim // 2
    inv = ROPE_THETA ** (-jnp.arange(half, dtype=jnp.float32) / half)
    ang = pos.astype(jnp.float32)[..., None] * inv
    cos = jnp.cos(ang)[:, :, None, :]
    sin = jnp.sin(ang)[:, :, None, :]
    xr = x[..., :rot_dim].astype(jnp.float32)
    x1, x2 = xr[..., :half], xr[..., half:]
    rot = jnp.concatenate([x1 * cos - x2 * sin, x2 * cos + x1 * sin], axis=-1).astype(x.dtype)
    return jnp.concatenate([rot, x[..., rot_dim:]], axis=-1)


def dilated_window_attention(q, k, v, window, dilation):
    b, s, h, e = q.shape
    span = window // dilation
    L = s // dilation
    nb = -(-L // BLK)
    Lp = nb * BLK

    def to_sub(t):
        t = t.reshape(b, L, dilation, h, e).transpose(0, 2, 1, 3, 4)
        return jnp.pad(t, ((0, 0), (0, 0), (0, Lp - L), (0, 0), (0, 0)))

    def band(t):
        t = jnp.pad(t, ((0, 0), (0, 0), (BLK, 0), (0, 0), (0, 0))).reshape(b, dilation, nb + 1, BLK, h, e)
        return jnp.concatenate([t[:, :, :-1], t[:, :, 1:]], axis=3)

    qb = to_sub(q).reshape(b, dilation, nb, BLK, h, e)
    kb = band(to_sub(k))
    vb = band(to_sub(v))
    scores = jnp.einsum('bdnqhe,bdnkhe->bdnhqk', qb, kb).astype(jnp.float32) * (e ** -0.5)
    qi = jnp.arange(BLK)[:, None]
    kj = jnp.arange(2 * BLK)[None, :]
    dist = qi + BLK - kj
    kpos = (jnp.arange(nb)[:, None, None] - 1) * BLK + kj[None]
    valid = (dist >= 0)[None] & (dist <= span)[None] & (kpos >= 0)
    scores = jnp.where(valid[None, None, :, None], scores, NEG_INF)
    m = jnp.max(scores, axis=-1, keepdims=True)
    p = jnp.exp(scores - m)
    den = jnp.sum(p, axis=-1)
    o = jnp.einsum('bdnhqk,bdnkhe->bdnqhe', p, vb.astype(jnp.float32))
    o = o / den.transpose(0, 1, 2, 4, 3)[..., None]
    lse = (m[..., 0] + jnp.log(den)).transpose(0, 1, 2, 4, 3)
    o = o.reshape(b, dilation, Lp, h, e)[:, :, :L].transpose(0, 2, 1, 3, 4).reshape(b, s, h, e)
    lse = lse.reshape(b, dilation, Lp, h)[:, :, :L].transpose(0, 2, 1, 3).reshape(b, s, h)
    return o, lse


def causal_attention_blocked(q, k, v):
    b, s, h, e = q.shape
    ev = v.shape[-1]
    nb = s // BLK
    scale = e ** -0.5
    qb = q.reshape(b, nb, BLK, h, e).transpose(1, 0, 2, 3, 4)
    kpos = jnp.arange(s)
    vf = v.astype(jnp.float32)

    def one_block(args):
        qblk, i = args
        sc = jnp.einsum('bqhe,bkhe->bhqk', qblk, k).astype(jnp.float32) * scale
        qpos = i * BLK + jnp.arange(BLK)
        sc = jnp.where(kpos[None, :] <= qpos[:, None], sc, NEG_INF)
        p = jax.nn.softmax(sc, axis=-1)
        return jnp.einsum('bhqk,bkhe->bqhe', p, vf)

    o = lax.map(one_block, (qb, jnp.arange(nb)))
    return o.transpose(1, 0, 2, 3, 4).reshape(b, s, h, ev)


def peer(hn, w_pq, sub_keys, u_emb, v_emb):
    b, s, d = hn.shape
    t = hn.reshape(b * s, d)
    q = (t @ w_pq).reshape(-1, PEER_HEADS, 2, PEER_QDIM // 2).astype(jnp.float32)
    sc = jnp.einsum('thpc,hpnc->thpn', q, sub_keys.astype(jnp.float32))
    vals, idx = lax.top_k(sc, PEER_TOPK)
    cand = vals[:, :, 0, :, None] + vals[:, :, 1, None, :]
    best, flat = lax.top_k(cand.reshape(cand.shape[0], PEER_HEADS, PEER_TOPK * PEER_TOPK), PEER_TOPK)
    i1 = jnp.take_along_axis(idx[:, :, 0], flat // PEER_TOPK, axis=-1)
    i2 = jnp.take_along_axis(idx[:, :, 1], flat % PEER_TOPK, axis=-1)
    expert = i1 * PEER_NKEYS + i2
    gate = jax.nn.softmax(best, axis=-1)
    nc = t.shape[0] // PEER_CHUNK

    def one_chunk(args):
        tc, ec, gc = args
        u_sel = u_emb[ec]
        act = jax.nn.gelu(jnp.einsum('td,thkd->thk', tc, u_sel).astype(jnp.float32), approximate=False)
        return jnp.einsum('thk,thkd->td', (act * gc).astype(tc.dtype), v_emb[ec])

    out = lax.map(one_chunk, (t.reshape(nc, PEER_CHUNK, d),
                              expert.reshape(nc, PEER_CHUNK, PEER_HEADS, PEER_TOPK),
                              gate.reshape(nc, PEER_CHUNK, PEER_HEADS, PEER_TOPK)))
    return out.reshape(b, s, d)


def setup_inputs(seed: int = 0) -> dict:
    key = jax.random.key(seed)
    ks = jax.random.split(key, 17)
    nrm = lambda k, shape, scale: jax.random.normal(k, shape, jnp.float32) * scale
    gain = lambda k, shape: 1.0 + 0.02 * jax.random.normal(k, shape, jnp.float32)
    x = jax.random.normal(ks[0], (BATCH, SEQ, D_MODEL), jnp.float32)
    offset = jax.random.randint(ks[1], (BATCH, 1), 0, 1024, dtype=jnp.int32)
    positions = (jnp.arange(SEQ, dtype=jnp.int32)[None, :] + offset).astype(jnp.int32)
    return {
        "x": x,
        "positions": positions,
        "g_mix": gain(ks[2], (DEPTH, D_MODEL)),
        "w_in": nrm(ks[3], (DEPTH, D_MODEL, IN_WIDTH), D_MODEL ** -0.5),
        "g_cq": gain(ks[4], (DEPTH, MLA_Q_RANK)),
        "w_uq": nrm(ks[5], (DEPTH, MLA_Q_RANK, B_HEADS * (MLA_NOPE + MLA_ROPE)), MLA_Q_RANK ** -0.5),
        "g_ckv": gain(ks[6], (DEPTH, MLA_KV_RANK)),
        "w_ukv": nrm(ks[7], (DEPTH, MLA_KV_RANK, B_HEADS * (MLA_NOPE + MLA_V)), MLA_KV_RANK ** -0.5),
        "w_out": nrm(ks[8], (DEPTH, MIX_WIDTH, D_MODEL), MIX_WIDTH ** -0.5),
        "g_ffn": gain(ks[9], (DEPTH, D_MODEL)),
        "w_pq": nrm(ks[10], (DEPTH, D_MODEL, PEER_HEADS * PEER_QDIM), D_MODEL ** -0.5),
        "sub_keys": nrm(ks[11], (DEPTH, PEER_HEADS, 2, PEER_NKEYS, PEER_QDIM // 2), (PEER_QDIM // 2) ** -0.5),
        "u_emb": nrm(ks[12], (DEPTH, PEER_EXPERTS, D_MODEL), D_MODEL ** -0.5),
        "v_emb": nrm(ks[13], (DEPTH, PEER_EXPERTS, D_MODEL), PEER_HEADS ** -0.5),
        "g_final": gain(ks[14], (D_MODEL,)),
    }


def reference(x, positions, g_mix, w_in, g_cq, w_uq, g_ckv, w_ukv, w_out, g_ffn,
              w_pq, sub_keys, u_emb, v_emb, g_final):
    b, s, _ = x.shape
    offs = np.cumsum(IN_SPLITS)[:-1].tolist()
    h = x
    for l in range(DEPTH):
        hn = rmsnorm(h, g_mix[l])
        proj = hn @ w_in[l]
        qa, ka, va, cq, ckv, kr = jnp.split(proj, offs, axis=-1)

        qa = rope(qa.reshape(b, s, A_HEADS, HEAD_DIM), positions, A_ROT_DIM)
        ka = rope(ka.reshape(b, s, A_HEADS, HEAD_DIM), positions, A_ROT_DIM)
        va = va.reshape(b, s, A_HEADS, HEAD_DIM)
        outs, lses = [], []
        for window, dilation in DIL_PATTERNS:
            o_i, lse_i = dilated_window_attention(qa, ka, va, window, dilation)
            outs.append(o_i)
            lses.append(lse_i)
        wts = jax.nn.softmax(jnp.stack(lses, axis=0), axis=0)
        o_a = jnp.einsum('pbsh,pbshe->bshe', wts, jnp.stack(outs, axis=0)).astype(h.dtype)

        qm = (rmsnorm(cq, g_cq[l]) @ w_uq[l]).reshape(b, s, B_HEADS, MLA_NOPE + MLA_ROPE)
        q_m = jnp.concatenate([qm[..., :MLA_NOPE], rope(qm[..., MLA_NOPE:], positions, MLA_ROPE)], axis=-1)
        kv = (rmsnorm(ckv, g_ckv[l]) @ w_ukv[l]).reshape(b, s, B_HEADS, MLA_NOPE + MLA_V)
        k_rope = rope(kr[:, :, None, :], positions, MLA_ROPE)
        k_m = jnp.concatenate([kv[..., :MLA_NOPE],
                               jnp.broadcast_to(k_rope, (b, s, B_HEADS, MLA_ROPE))], axis=-1)
        o_m = causal_attention_blocked(q_m, k_m, kv[..., MLA_NOPE:]).astype(h.dtype)

        mixed = jnp.concatenate([o_a.reshape(b, s, A_WIDTH), o_m.reshape(b, s, B_WIDTH)], axis=-1)
        h = h + mixed @ w_out[l]

        h = h + peer(rmsnorm(h, g_ffn[l]), w_pq[l], sub_keys[l], u_emb[l], v_emb[l])
    return rmsnorm(h, g_final)
```

```python
import functools
import math

import jax
import jax.numpy as jnp
import numpy as np
from jax import lax
from jax.experimental import pallas as pl
from jax.experimental.pallas import tpu as pltpu

F32 = jnp.float32
BF16 = jnp.bfloat16
I32 = jnp.int32

ROPE_THETA = 500000.0
NORM_EPS = 1e-6
NEG_INF = -1e30
BLK = 128
A_HEADS = 8
HEAD_DIM = 64
A_ROT_DIM = HEAD_DIM // 4
DIL_PATTERNS = ((128, 1), (512, 4), (2048, 16))
A_WIDTH = A_HEADS * HEAD_DIM
B_HEADS = 8
MLA_Q_RANK = 256
MLA_KV_RANK = 128
MLA_NOPE = 64
MLA_ROPE = 32
MLA_V = 64
B_WIDTH = B_HEADS * MLA_V
PEER_HEADS = 8
PEER_NKEYS = 128
PEER_QDIM = 256
PEER_TOPK = 16
PEER_SEL = PEER_HEADS * PEER_TOPK

V7X_LANES = 128
V7X_SUBLANES = 8
V7X_VMEM_BYTES = 64 * 1024 * 1024
V7X_VMEM_REQUEST_CAP = 56 * 1024 * 1024


def _vmem_limit(pipelined_bytes, resident_bytes=0):
    est = 2 * int(pipelined_bytes) + int(resident_bytes)
    return int(min(max(est, 16 * 1024 * 1024), V7X_VMEM_REQUEST_CAP))


def _nbytes(shape, dtype):
    return int(np.prod(shape)) * jnp.dtype(dtype).itemsize


def _dot(a, b):
    return jnp.dot(a, b, preferred_element_type=F32)


def _dot_nt(a, b):
    return lax.dot_general(a, b, (((1,), (1,)), ((), ())), preferred_element_type=F32)


def _rms(x, g):
    ms = jnp.mean(x * x, axis=-1, keepdims=True)
    return x * lax.rsqrt(ms + NORM_EPS) * g


def _rope_tables(pos_f32, inv_row, lo_start, half, period):
    ang = pos_f32 * inv_row
    cos, sin = jnp.cos(ang), jnp.sin(ang)
    lane = lax.broadcasted_iota(I32, (1, V7X_LANES), 1) & (period - 1)
    is_lo = (lane >= lo_start) & (lane < lo_start + half)
    is_hi = (lane >= lo_start + half) & (lane < lo_start + 2 * half)
    c = jnp.where(is_lo | is_hi, cos, 1.0)
    s_lo = jnp.where(is_lo, -sin, 0.0)
    s_hi = jnp.where(is_hi, sin, 0.0)
    return c, s_lo, s_hi


def _apply_rope(x, tables, half):
    c, s_lo, s_hi = tables
    return (x * c + pltpu.roll(x, V7X_LANES - half, 1) * s_lo
            + pltpu.roll(x, half, 1) * s_hi)


def _mix_in_kernel(x_ref, pos_ref, gmix_ref, wa_ref, wcq_ref, wckv_ref, wkr_ref,
                   gcq_ref, wuq_ref, gckv_ref, wuk_ref, wuv_ref, inva_ref, invm_ref,
                   qa_ref, ka_ref, va_ref, qm_ref, km_ref, vm_ref, *, a_scale, m_scale):
    hn = _rms(x_ref[...], gmix_ref[...]).astype(BF16)
    pos = pos_ref[...].astype(F32)
    tab_a = _rope_tables(pos, inva_ref[...], 0, A_ROT_DIM // 2, HEAD_DIM)
    tab_m = _rope_tables(pos, invm_ref[...], MLA_NOPE, MLA_ROPE // 2, V7X_LANES)

    proj = _dot(hn, wa_ref[...])
    for c in range(A_WIDTH // V7X_LANES):
        sl = slice(c * V7X_LANES, (c + 1) * V7X_LANES)
        q = proj[:, sl]
        k = proj[:, A_WIDTH + c * V7X_LANES:A_WIDTH + (c + 1) * V7X_LANES]
        qa_ref[:, sl] = (_apply_rope(q, tab_a, A_ROT_DIM // 2) * a_scale).astype(BF16)
        ka_ref[:, sl] = _apply_rope(k, tab_a, A_ROT_DIM // 2).astype(BF16)
    va_ref[...] = proj[:, 2 * A_WIDTH:].astype(BF16)

    cq = _rms(_dot(hn, wcq_ref[...]), gcq_ref[...]).astype(BF16)
    qm = _dot(cq, wuq_ref[...])
    ckv = _rms(_dot(hn, wckv_ref[...]), gckv_ref[...]).astype(BF16)
    kn = _dot(ckv, wuk_ref[...])
    kr = _apply_rope(_dot(hn, wkr_ref[...]), tab_m, MLA_ROPE // 2)
    for h in range(B_HEADS):
        sl = slice(h * V7X_LANES, (h + 1) * V7X_LANES)
        qm_ref[:, sl] = (_apply_rope(qm[:, sl], tab_m, MLA_ROPE // 2) * m_scale).astype(BF16)
        km_ref[:, sl] = (kn[:, sl] + kr).astype(BF16)
    vm_ref[...] = _dot(ckv, wuv_ref[...]).astype(BF16)


def _mix_in(x2, pos2, g_mix, w_in, g_cq, w_uq, g_ckv, w_ukv):
    T, D = x2.shape
    tm = min(512, T)
    nope_rope = MLA_NOPE + MLA_ROPE
    wa = w_in[:, :3 * A_WIDTH].astype(BF16)
    o0 = 3 * A_WIDTH
    wcq = w_in[:, o0:o0 + MLA_Q_RANK].astype(BF16)
    wckv = w_in[:, o0 + MLA_Q_RANK:o0 + MLA_Q_RANK + MLA_KV_RANK].astype(BF16)
    wkr_raw = w_in[:, o0 + MLA_Q_RANK + MLA_KV_RANK:]
    wkr = jnp.pad(wkr_raw, ((0, 0), (MLA_NOPE, V7X_LANES - nope_rope))).astype(BF16)
    wuq = jnp.pad(w_uq.reshape(MLA_Q_RANK, B_HEADS, nope_rope),
                  ((0, 0), (0, 0), (0, V7X_LANES - nope_rope))
                  ).reshape(MLA_Q_RANK, B_HEADS * V7X_LANES).astype(BF16)
    wukv = w_ukv.reshape(MLA_KV_RANK, B_HEADS, MLA_NOPE + MLA_V)
    wuk = jnp.pad(wukv[:, :, :MLA_NOPE], ((0, 0), (0, 0), (0, V7X_LANES - MLA_NOPE))
                  ).reshape(MLA_KV_RANK, B_HEADS * V7X_LANES).astype(BF16)
    wuv = wukv[:, :, MLA_NOPE:].reshape(MLA_KV_RANK, B_WIDTH).astype(BF16)
    lane = jnp.arange(V7X_LANES)
    ha, hm = A_ROT_DIM // 2, MLA_ROPE // 2
    inv_a = (ROPE_THETA ** (-jnp.arange(ha, dtype=F32) / ha))[lane % ha].reshape(1, V7X_LANES)
    inv_m = (ROPE_THETA ** (-jnp.arange(hm, dtype=F32) / hm))[lane % hm].reshape(1, V7X_LANES)

    full = lambda a: pl.BlockSpec(a.shape, lambda i: (0,) * a.ndim)
    row = lambda w: pl.BlockSpec((tm, w), lambda i: (i, 0))
    consts = (g_mix.reshape(1, D), wa, wcq, wckv, wkr, g_cq.reshape(1, -1), wuq,
              g_ckv.reshape(1, -1), wuk, wuv, inv_a, inv_m)
    out_widths = (A_WIDTH, A_WIDTH, A_WIDTH, B_HEADS * V7X_LANES, B_HEADS * V7X_LANES, B_WIDTH)
    blocks = (_nbytes((tm, D), F32) + sum(_nbytes(c.shape, c.dtype) for c in consts)
              + sum(_nbytes((tm, w), BF16) for w in out_widths))
    temps = _nbytes((tm, 3 * A_WIDTH + 4 * B_HEADS * V7X_LANES), F32)
    return pl.pallas_call(
        functools.partial(_mix_in_kernel, a_scale=HEAD_DIM ** -0.5, m_scale=nope_rope ** -0.5),
        grid=(T // tm,),
        in_specs=[row(D), pl.BlockSpec((tm, 1), lambda i: (i, 0))] + [full(c) for c in consts],
        out_specs=[row(w) for w in out_widths],
        out_shape=[jax.ShapeDtypeStruct((T, w), BF16) for w in out_widths],
        compiler_params=pltpu.CompilerParams(
            dimension_semantics=("parallel",), vmem_limit_bytes=_vmem_limit(blocks, temps)),
        name="mix_in",
    )(x2, pos2, *consts)


def _dilated_kernel(q_ref, kp_ref, kc_ref, vp_ref, vc_ref, o_ref, lse_ref, *, groups, span):
    n = pl.program_id(2)
    lane = lax.broadcasted_iota(I32, (1, V7X_LANES), 1)
    lo = lane < HEAD_DIM
    qi = lax.broadcasted_iota(I32, (BLK, 2 * BLK), 0)
    kj = lax.broadcasted_iota(I32, (BLK, 2 * BLK), 1)
    dist = qi + BLK - kj
    band = (dist >= 0) & (dist <= span)
    first_valid = band & (kj >= jnp.where(n > 0, 0, BLK))
    for hp in range(A_WIDTH // V7X_LANES):
        sl = slice(hp * V7X_LANES, (hp + 1) * V7X_LANES)
        kc = kc_ref[:, sl]
        vc = vc_ref[:, sl]
        for j in range(groups):
            rows = slice(j * BLK, (j + 1) * BLK)
            q2 = q_ref[rows, sl]
            if j == 0:
                k2 = jnp.concatenate([kp_ref[:, sl], kc[:BLK]], axis=0)
                v2 = jnp.concatenate([vp_ref[:, sl], vc[:BLK]], axis=0)
                valid = first_valid
            else:
                k2 = kc[(j - 1) * BLK:(j + 1) * BLK]
                v2 = vc[(j - 1) * BLK:(j + 1) * BLK]
                valid = band
            acc = jnp.zeros((BLK, V7X_LANES), F32)
            lse2 = jnp.zeros((BLK, V7X_LANES), F32)
            for h in range(2):
                sel = lo if h == 0 else ~lo
                qh = jnp.where(sel, q2, jnp.zeros_like(q2))
                vh = jnp.where(sel, v2, jnp.zeros_like(v2))
                s = jnp.where(valid, _dot_nt(qh, k2), NEG_INF)
                m = jnp.max(s, axis=-1, keepdims=True)
                p = jnp.exp(s - m)
                den = jnp.sum(p, axis=-1, keepdims=True)
                acc = acc + _dot(p.astype(BF16), vh) / den
                lse2 = jnp.where(sel, m + jnp.log(den), lse2)
            o_ref[rows, sl] = acc
            lse_ref[rows, sl] = lse2


def _dilated(qa, ka, va, B, S, window, dilation):
    span = window // dilation
    assert span <= BLK
    L = S // dilation
    assert L * dilation == S and L % BLK == 0
    nblk = L // BLK
    groups = math.gcd(nblk, 4)
    W = A_WIDTH
    view = lambda a: a.reshape(B, L, dilation * W)
    cur = pl.BlockSpec((None, groups * BLK, W), lambda b, r, n: (b, n, r))
    prev = pl.BlockSpec((None, BLK, W), lambda b, r, n: (b, jnp.maximum(n * groups - 1, 0), r))
    blocks = 3 * _nbytes((groups * BLK, W), BF16) + 2 * _nbytes((BLK, W), BF16) \
        + 2 * _nbytes((groups * BLK, W), F32)
    o, lse = pl.pallas_call(
        functools.partial(_dilated_kernel, groups=groups, span=span),
        grid=(B, dilation, nblk // groups),
        in_specs=[cur, prev, cur, prev, cur],
        out_specs=[cur, cur],
        out_shape=[jax.ShapeDtypeStruct((B, L, dilation * W), F32)] * 2,
        compiler_params=pltpu.CompilerParams(
            dimension_semantics=("parallel", "parallel", "arbitrary"),
            vmem_limit_bytes=_vmem_limit(blocks, 4 * 1024 * 1024)),
        name=f"dilated_d{dilation}",
    )(view(qa), view(ka), view(ka), view(va), view(va))
    return o.reshape(B * S, W), lse.reshape(B * S, W)


def _mla_kernel(q_ref, k_ref, v_ref, o_ref, *, tq, tk):
    qi = pl.program_id(2)
    lane = lax.broadcasted_iota(I32, (1, V7X_LANES), 1)
    lo = lane < MLA_V
    qs = [q_ref[:, h * V7X_LANES:(h + 1) * V7X_LANES] for h in range(2)]
    q_start = qi * tq

    def step(ki, carry, masked):
        ms, ls, acc = carry
        start = pl.multiple_of(ki * tk, tk)
        kb = k_ref[pl.ds(start, tk), :]
        vb = v_ref[pl.ds(start, tk), :]
        new_ms, new_ls, alphas, pvs = [], [], [], []
        for h in range(2):
            s = _dot_nt(qs[h], kb[:, h * V7X_LANES:(h + 1) * V7X_LANES])
            if masked:
                qpos = q_start + lax.broadcasted_iota(I32, (tq, tk), 0)
                kpos = start + lax.broadcasted_iota(I32, (tq, tk), 1)
                s = jnp.where(kpos <= qpos, s, NEG_INF)
            m_new = jnp.maximum(ms[h], jnp.max(s, axis=-1, keepdims=True))
            alpha = jnp.exp(ms[h] - m_new)
            p = jnp.exp(s - m_new)
            new_ls.append(alpha * ls[h] + jnp.sum(p, axis=-1, keepdims=True))
            new_ms.append(m_new)
            alphas.append(alpha)
            vh = jnp.where(lo if h == 0 else ~lo, vb, jnp.zeros_like(vb))
            pvs.append(_dot(p.astype(BF16), vh))
        acc = acc * jnp.where(lo, alphas[0], alphas[1]) + pvs[0] + pvs[1]
        return tuple(new_ms), tuple(new_ls), acc

    init = ((jnp.full((tq, 1), NEG_INF, F32),) * 2, (jnp.zeros((tq, 1), F32),) * 2,
            jnp.zeros((tq, V7X_LANES), F32))
    n_full = q_start // tk
    carry = lax.fori_loop(0, n_full, lambda ki, c: step(ki, c, False), init)
    _, ls, acc = step(n_full, carry, True)
    o_ref[...] = (acc / jnp.where(lo, ls[0], ls[1])).astype(o_ref.dtype)


def _mla(qm, km, vm, B, S):
    tq = min(256, S)
    tk = min(512, S)
    assert tk % tq == 0 and S % tk == 0
    hw = 2 * V7X_LANES
    q3, k3, v3 = (a.reshape(B, S, -1) for a in (qm, km, vm))
    blocks = _nbytes((tq, hw), BF16) + _nbytes((S, hw), BF16) + _nbytes((S, V7X_LANES), BF16) \
        + _nbytes((tq, V7X_LANES), BF16)
    temps = 6 * _nbytes((tq, tk), F32)
    out = pl.pallas_call(
        functools.partial(_mla_kernel, tq=tq, tk=tk),
        grid=(B, B_HEADS // 2, S // tq),
        in_specs=[pl.BlockSpec((None, tq, hw), lambda b, h, i: (b, i, h)),
                  pl.BlockSpec((None, S, hw), lambda b, h, i: (b, 0, h)),
                  pl.BlockSpec((None, S, V7X_LANES), lambda b, h, i: (b, 0, h))],
        out_specs=pl.BlockSpec((None, tq, V7X_LANES), lambda b, h, i: (b, i, h)),
        out_shape=jax.ShapeDtypeStruct((B, S, B_WIDTH), BF16),
        compiler_params=pltpu.CompilerParams(
            dimension_semantics=("parallel", "parallel", "arbitrary"),
            vmem_limit_bytes=_vmem_limit(blocks, temps)),
        name="mla_attention",
    )(q3, k3, v3)
    return out.reshape(B * S, B_WIDTH)


def _mix_out_kernel(o1, o2, o3, l1, l2, l3, om_ref, x_ref, wo_ref, gffn_ref, wpq_ref,
                    h_ref, hn_ref, qp_ref):
    lses = [l1[...], l2[...], l3[...]]
    mx = jnp.maximum(jnp.maximum(lses[0], lses[1]), lses[2])
    es = [jnp.exp(l - mx) for l in lses]
    oa = (es[0] * o1[...] + es[1] * o2[...] + es[2] * o3[...]) / (es[0] + es[1] + es[2])
    mixed = _dot(oa.astype(BF16), wo_ref[:A_WIDTH, :]) + _dot(om_ref[...], wo_ref[A_WIDTH:, :])
    h = x_ref[...] + mixed
    h_ref[...] = h
    hn = _rms(h, gffn_ref[...]).astype(BF16)
    hn_ref[...] = hn
    qp_ref[...] = _dot(hn, wpq_ref[...])


def _mix_out(os_, lses, om, x2, w_out, g_ffn, w_pq):
    T, D = x2.shape
    tm = min(256, T)
    wo = w_out.astype(BF16)
    wpq = w_pq.astype(BF16)
    QW = wpq.shape[1]
    row = lambda w: pl.BlockSpec((tm, w), lambda i: (i, 0))
    full = lambda a: pl.BlockSpec(a.shape, lambda i: (0,) * a.ndim)
    blocks = 6 * _nbytes((tm, A_WIDTH), F32) + _nbytes((tm, B_WIDTH), BF16) \
        + 2 * _nbytes((tm, D), F32) + _nbytes(wo.shape, BF16) + _nbytes(wpq.shape, BF16) \
        + _nbytes((tm, D), BF16) + _nbytes((tm, QW), F32)
    return pl.pallas_call(
        _mix_out_kernel,
        grid=(T // tm,),
        in_specs=[row(A_WIDTH)] * 6 + [row(B_WIDTH), row(D), full(wo),
                                       pl.BlockSpec((1, D), lambda i: (0, 0)), full(wpq)],
        out_specs=[row(D), row(D), row(QW)],
        out_shape=[jax.ShapeDtypeStruct((T, D), F32), jax.ShapeDtypeStruct((T, D), BF16),
                   jax.ShapeDtypeStruct((T, QW), F32)],
        compiler_params=pltpu.CompilerParams(
            dimension_semantics=("parallel",),
            vmem_limit_bytes=_vmem_limit(blocks, 4 * _nbytes((tm, QW), F32))),
        name="mix_out",
    )(*os_, *lses, om, x2, wo, g_ffn.reshape(1, D), wpq)


def _topk_rows(sc, k, payload=None):
    n = sc.shape[0]
    rows = lax.broadcasted_iota(I32, sc.shape, 0)
    out_rows = lax.broadcasted_iota(I32, (k, sc.shape[1]), 0)
    vals = jnp.zeros((k, sc.shape[1]), F32)
    sel = jnp.zeros((k, sc.shape[1]), I32)
    for i in range(k):
        m = jnp.max(sc, axis=0, keepdims=True)
        idx = jnp.min(jnp.where(sc == m, rows, n), axis=0, keepdims=True)
        hit = rows == idx
        if payload is None:
            got = idx
        else:
            got = jnp.max(jnp.where(hit, payload, -1), axis=0, keepdims=True)
        vals = jnp.where(out_rows == i, m, vals)
        sel = jnp.where(out_rows == i, got, sel)
        sc = jnp.where(hit, -jnp.inf, sc)
    return vals, sel


def _peer_topk_kernel(qp_ref, keys_ref, i1_ref, i2_ref, gate_ref):
    K = PEER_TOPK
    for h in range(PEER_HEADS):
        vals, idxs = [], []
        for p in range(2):
            c0 = (2 * h + p) * (PEER_QDIM // 2)
            q = qp_ref[:, c0:c0 + PEER_QDIM // 2].astype(BF16)
            sc = _dot_nt(keys_ref[2 * h + p], q)
            v, i = _topk_rows(sc, K)
            vals.append(v)
            idxs.append(i)
        cand = jnp.concatenate([vals[0][r:r + 1, :] + vals[1] for r in range(K)], axis=0)
        pair = jnp.concatenate([idxs[0][r:r + 1, :] * PEER_NKEYS + idxs[1] for r in range(K)],
                               axis=0)
        best, expert = _topk_rows(cand, K, payload=pair)
        e = jnp.exp(best - jnp.max(best, axis=0, keepdims=True))
        rows = slice(h * K, (h + 1) * K)
        gate_ref[rows, :] = e / jnp.sum(e, axis=0, keepdims=True)
        i1_ref[rows, :] = expert // PEER_NKEYS
        i2_ref[rows, :] = expert % PEER_NKEYS


def _peer_topk(qp, sub_keys):
    T, QW = qp.shape
    tt = min(256, T)
    keys = sub_keys.reshape(PEER_HEADS * 2, PEER_NKEYS, PEER_QDIM // 2).astype(BF16)
    col = pl.BlockSpec((PEER_SEL, tt), lambda i: (0, i))
    blocks = _nbytes((tt, QW), F32) + _nbytes(keys.shape, BF16) + 3 * _nbytes((PEER_SEL, tt), F32)
    return pl.pallas_call(
        _peer_topk_kernel,
        grid=(T // tt,),
        in_specs=[pl.BlockSpec((tt, QW), lambda i: (i, 0)),
                  pl.BlockSpec(keys.shape, lambda i: (0, 0, 0))],
        out_specs=[col, col, col],
        out_shape=[jax.ShapeDtypeStruct((PEER_SEL, T), I32), jax.ShapeDtypeStruct((PEER_SEL, T), I32),
                   jax.ShapeDtypeStruct((PEER_SEL, T), F32)],
        compiler_params=pltpu.CompilerParams(
            dimension_semantics=("parallel",),
            vmem_limit_bytes=_vmem_limit(blocks, 16 * _nbytes((2 * PEER_SEL, tt), F32))),
        name="peer_topk",
    )(qp, keys)


def _peer_act_kernel(hn_ref, u_ref, i1_ref, i2_ref, gate_ref, coef_ref, z_ref, *, chunks):
    g = pl.program_id(1)

    @pl.when(g == 0)
    def _():
        z_ref[...] = jnp.zeros_like(z_ref)

    z = _dot_nt(hn_ref[...], u_ref[...])
    i1 = i1_ref[...]
    i2 = i2_ref[...]
    zsel = z_ref[...]
    for j in range(chunks):
        zc = z[:, j * PEER_NKEYS:(j + 1) * PEER_NKEYS]
        picked = jnp.take_along_axis(zc, i2, axis=1)
        zsel = jnp.where(i1 == g * chunks + j, picked, zsel)
    z_ref[...] = zsel

    @pl.when(g == pl.num_programs(1) - 1)
    def _():
        act = 0.5 * zsel * (1.0 + lax.erf(zsel * np.float32(math.sqrt(0.5))))
        coef_ref[...] = act * gate_ref[...]


def _peer_act(hn, u_bf, i1, i2, gate):
    T, D = hn.shape
    E = u_bf.shape[0]
    tm = min(1024, T)
    chunks = 2
    rows = chunks * PEER_NKEYS
    sel = pl.BlockSpec((tm, PEER_SEL), lambda i, g: (i, 0))
    blocks = _nbytes((tm, D), BF16) + _nbytes((rows, D), BF16) + 4 * _nbytes((tm, PEER_SEL), F32)
    return pl.pallas_call(
        functools.partial(_peer_act_kernel, chunks=chunks),
        grid=(T // tm, E // rows),
        in_specs=[pl.BlockSpec((tm, D), lambda i, g: (i, 0)),
                  pl.BlockSpec((rows, D), lambda i, g: (g, 0)), sel, sel, sel],
        out_specs=sel,
        out_shape=jax.ShapeDtypeStruct((T, PEER_SEL), F32),
        scratch_shapes=[pltpu.VMEM((tm, PEER_SEL), F32)],
        compiler_params=pltpu.CompilerParams(
            dimension_semantics=("parallel", "arbitrary"),
            vmem_limit_bytes=_vmem_limit(blocks, 6 * _nbytes((tm, rows), F32))),
        name="peer_act",
    )(hn, u_bf, i1, i2, gate)


def _peer_out_kernel(i1_ref, i2_ref, coef_ref, v_ref, h_ref, gfin_ref, out_ref,
                     sx_ref, acc_ref, *, tt, pitch, chunks):
    g = pl.program_id(1)
    NK = PEER_NKEYS

    @pl.when(g == 0)
    def _():
        acc_ref[...] = jnp.zeros_like(acc_ref)
        sub = lax.broadcasted_iota(I32, (NK, PEER_SEL), 0)

        def scatter_token(t, carry):
            i1r = i1_ref[pl.ds(t, 1), :]
            i2r = i2_ref[pl.ds(t, 1), :]
            cr = coef_ref[pl.ds(t, 1), :]
            a_t = jnp.where(sub == i1r, 1.0, 0.0).astype(BF16)
            b_t = jnp.where(sub == i2r, cr, 0.0).astype(BF16)
            s = _dot_nt(a_t, b_t)
            sx_ref[pl.ds(t, NK, stride=pitch), :] = s
            return carry

        lax.fori_loop(0, tt, scatter_token, 0)

    part = jnp.zeros(acc_ref.shape, F32)
    for cc in range(chunks // 2):
        c0 = g * chunks + 2 * cc
        lhs = jnp.concatenate(
            [sx_ref[pl.ds(pl.multiple_of((c0 + k) * pitch, V7X_SUBLANES), tt), :] for k in range(2)],
            axis=1).astype(BF16)
        part = part + _dot(lhs, v_ref[2 * cc * NK:(2 * cc + 2) * NK, :])
    acc_ref[...] += part

    @pl.when(g == pl.num_programs(1) - 1)
    def _():
        out_ref[...] = _rms(h_ref[...] + acc_ref[...], gfin_ref[...])


def _peer_out(i1, i2, coef, v_bf, h, g_final):
    T, D = h.shape
    E = v_bf.shape[0]
    tt = min(256, T)
    chunks = 16
    pitch = tt + V7X_SUBLANES
    rows = chunks * PEER_NKEYS
    sel = pl.BlockSpec((tt, PEER_SEL), lambda i, g: (i, 0))
    tok = pl.BlockSpec((tt, D), lambda i, g: (i, 0))
    blocks = 3 * _nbytes((tt, PEER_SEL), F32) + _nbytes((rows, D), BF16) + 2 * _nbytes((tt, D), F32)
    scratch = _nbytes((PEER_NKEYS * pitch, PEER_NKEYS), F32) + _nbytes((tt, D), F32)
    return pl.pallas_call(
        functools.partial(_peer_out_kernel, tt=tt, pitch=pitch, chunks=chunks),
        grid=(T // tt, E // rows),
        in_specs=[sel, sel, sel, pl.BlockSpec((rows, D), lambda i, g: (g, 0)), tok,
                  pl.BlockSpec((1, D), lambda i, g: (0, 0))],
        out_specs=tok,
        out_shape=jax.ShapeDtypeStruct((T, D), F32),
        scratch_shapes=[pltpu.VMEM((PEER_NKEYS * pitch, PEER_NKEYS), F32),
                        pltpu.VMEM((tt, D), F32)],
        compiler_params=pltpu.CompilerParams(
            dimension_semantics=("parallel", "arbitrary"),
            vmem_limit_bytes=_vmem_limit(blocks, scratch + 4 * _nbytes((tt, D), F32))),
        name="peer_out",
    )(i1, i2, coef, v_bf, h, g_final.reshape(1, D))


def kernel(x, positions, g_mix, w_in, g_cq, w_uq, g_ckv, w_ukv, w_out, g_ffn,
           w_pq, sub_keys, u_emb, v_emb, g_final):
    B, S, D = x.shape
    T = B * S
    assert g_mix.shape[0] == 1, "single-layer configuration"
    x2 = x.reshape(T, D)
    pos2 = positions.reshape(T, 1)

    qa, ka, va, qm, km, vm = _mix_in(x2, pos2, g_mix[0], w_in[0], g_cq[0], w_uq[0],
                                     g_ckv[0], w_ukv[0])
    outs, lses = zip(*[_dilated(qa, ka, va, B, S, w, d) for w, d in DIL_PATTERNS])
    om = _mla(qm, km, vm, B, S)
    h, hn, qp = _mix_out(outs, lses, om, x2, w_out[0], g_ffn[0], w_pq[0])
    i1, i2, gate = _peer_topk(qp, sub_keys[0])
    i1, i2, gate = i1.T, i2.T, gate.T
    coef = _peer_act(hn, u_emb[0].astype(BF16), i1, i2, gate)
    out = _peer_out(i1, i2, coef, v_emb[0].astype(BF16), h, g_final)
    return out.reshape(B, S, D)
```

```python
import functools
import math

import jax
import jax.numpy as jnp
import numpy as np
from jax import lax
from jax.experimental import pallas as pl
from jax.experimental.pallas import tpu as pltpu

F32 = jnp.float32
BF16 = jnp.bfloat16
I32 = jnp.int32

ROPE_THETA = 500000.0
NORM_EPS = 1e-6
NEG_INF = -1e30
BLK = 128
A_HEADS = 8
HEAD_DIM = 64
A_ROT_DIM = HEAD_DIM // 4
DIL_PATTERNS = ((128, 1), (512, 4), (2048, 16))
A_WIDTH = A_HEADS * HEAD_DIM
B_HEADS = 8
MLA_Q_RANK = 256
MLA_KV_RANK = 128
MLA_NOPE = 64
MLA_ROPE = 32
MLA_V = 64
B_WIDTH = B_HEADS * MLA_V
PEER_HEADS = 8
PEER_NKEYS = 128
PEER_QDIM = 256
PEER_TOPK = 16
PEER_SEL = PEER_HEADS * PEER_TOPK
SCATTER_UNROLL = 16
MLA_TQ = 512
MLA_TK = 512

V7X_LANES = 128
V7X_SUBLANES = 8
V7X_VMEM_BYTES = 64 * 1024 * 1024
V7X_VMEM_REQUEST_CAP = 56 * 1024 * 1024


def _vmem_limit(pipelined_bytes, resident_bytes=0):
    est = 2 * int(pipelined_bytes) + int(resident_bytes)
    return int(min(max(est, 16 * 1024 * 1024), V7X_VMEM_REQUEST_CAP))


def _nbytes(shape, dtype):
    return int(np.prod(shape)) * jnp.dtype(dtype).itemsize


def _dot(a, b):
    return jnp.dot(a, b, preferred_element_type=F32)


def _dot_nt(a, b):
    return lax.dot_general(a, b, (((1,), (1,)), ((), ())), preferred_element_type=F32)


def _rms(x, g):
    ms = jnp.mean(x * x, axis=-1, keepdims=True)
    return x * lax.rsqrt(ms + NORM_EPS) * g


def _rope_tables(pos_f32, inv_row, lo_start, half, period):
    ang = pos_f32 * inv_row
    cos, sin = jnp.cos(ang), jnp.sin(ang)
    lane = lax.broadcasted_iota(I32, (1, V7X_LANES), 1) & (period - 1)
    is_lo = (lane >= lo_start) & (lane < lo_start + half)
    is_hi = (lane >= lo_start + half) & (lane < lo_start + 2 * half)
    c = jnp.where(is_lo | is_hi, cos, 1.0)
    s_lo = jnp.where(is_lo, -sin, 0.0)
    s_hi = jnp.where(is_hi, sin, 0.0)
    return c, s_lo, s_hi


def _apply_rope(x, tables, half):
    c, s_lo, s_hi = tables
    return (x * c + pltpu.roll(x, V7X_LANES - half, 1) * s_lo
            + pltpu.roll(x, half, 1) * s_hi)


def _mix_in_kernel(x_ref, pos_ref, gmix_ref, wa_ref, wcq_ref, wckv_ref, wkr_ref,
                   gcq_ref, wuq_ref, gckv_ref, wuk_ref, wuvt_ref, inva_ref, invm_ref,
                   qa_ref, ka_ref, va_ref, qm_ref, km_ref, vmt_ref, *, a_scale, m_scale):
    hn = _rms(x_ref[...], gmix_ref[...]).astype(BF16)
    pos = pos_ref[...].astype(F32)
    tab_a = _rope_tables(pos, inva_ref[...], 0, A_ROT_DIM // 2, HEAD_DIM)
    tab_m = _rope_tables(pos, invm_ref[...], MLA_NOPE, MLA_ROPE // 2, V7X_LANES)

    proj = _dot(hn, wa_ref[...])
    for c in range(A_WIDTH // V7X_LANES):
        sl = slice(c * V7X_LANES, (c + 1) * V7X_LANES)
        q = proj[:, sl]
        k = proj[:, A_WIDTH + c * V7X_LANES:A_WIDTH + (c + 1) * V7X_LANES]
        qa_ref[:, sl] = (_apply_rope(q, tab_a, A_ROT_DIM // 2) * a_scale).astype(BF16)
        ka_ref[:, sl] = _apply_rope(k, tab_a, A_ROT_DIM // 2).astype(BF16)
    va_ref[...] = proj[:, 2 * A_WIDTH:].astype(BF16)

    cq = _rms(_dot(hn, wcq_ref[...]), gcq_ref[...]).astype(BF16)
    qm = _dot(cq, wuq_ref[...])
    ckv = _rms(_dot(hn, wckv_ref[...]), gckv_ref[...]).astype(BF16)
    kn = _dot(ckv, wuk_ref[...])
    kr = _apply_rope(_dot(hn, wkr_ref[...]), tab_m, MLA_ROPE // 2)
    for h in range(B_HEADS):
        sl = slice(h * V7X_LANES, (h + 1) * V7X_LANES)
        qm_ref[:, sl] = (_apply_rope(qm[:, sl], tab_m, MLA_ROPE // 2) * m_scale).astype(BF16)
        km_ref[:, sl] = (kn[:, sl] + kr).astype(BF16)
    vmt_ref[...] = _dot_nt(wuvt_ref[...], ckv).astype(BF16).reshape(vmt_ref.shape)


def _mix_in(x2, pos2, g_mix, w_in, g_cq, w_uq, g_ckv, w_ukv, tm):
    T, D = x2.shape
    nope_rope = MLA_NOPE + MLA_ROPE
    wa = w_in[:, :3 * A_WIDTH].astype(BF16)
    o0 = 3 * A_WIDTH
    wcq = w_in[:, o0:o0 + MLA_Q_RANK].astype(BF16)
    wckv = w_in[:, o0 + MLA_Q_RANK:o0 + MLA_Q_RANK + MLA_KV_RANK].astype(BF16)
    wkr_raw = w_in[:, o0 + MLA_Q_RANK + MLA_KV_RANK:]
    wkr = jnp.pad(wkr_raw, ((0, 0), (MLA_NOPE, V7X_LANES - nope_rope))).astype(BF16)
    wuq = jnp.pad(w_uq.reshape(MLA_Q_RANK, B_HEADS, nope_rope),
                  ((0, 0), (0, 0), (0, V7X_LANES - nope_rope))
                  ).reshape(MLA_Q_RANK, B_HEADS * V7X_LANES).astype(BF16)
    wukv = w_ukv.reshape(MLA_KV_RANK, B_HEADS, MLA_NOPE + MLA_V)
    wuk = jnp.pad(wukv[:, :, :MLA_NOPE], ((0, 0), (0, 0), (0, V7X_LANES - MLA_NOPE))
                  ).reshape(MLA_KV_RANK, B_HEADS * V7X_LANES).astype(BF16)
    wuvt = wukv[:, :, MLA_NOPE:].reshape(MLA_KV_RANK, B_WIDTH).T.astype(BF16)
    lane = jnp.arange(V7X_LANES)
    ha, hm = A_ROT_DIM // 2, MLA_ROPE // 2
    inv_a = (ROPE_THETA ** (-jnp.arange(ha, dtype=F32) / ha))[lane % ha].reshape(1, V7X_LANES)
    inv_m = (ROPE_THETA ** (-jnp.arange(hm, dtype=F32) / hm))[lane % hm].reshape(1, V7X_LANES)

    full = lambda a: pl.BlockSpec(a.shape, lambda i: (0,) * a.ndim)
    row = lambda w: pl.BlockSpec((tm, w), lambda i: (i, 0))
    consts = (g_mix.reshape(1, D), wa, wcq, wckv, wkr, g_cq.reshape(1, -1), wuq,
              g_ckv.reshape(1, -1), wuk, wuvt, inv_a, inv_m)
    out_widths = (A_WIDTH, A_WIDTH, A_WIDTH, B_HEADS * V7X_LANES, B_HEADS * V7X_LANES)
    vt_shape = (B_HEADS // 2, 2 * MLA_V, tm)
    blocks = (_nbytes((tm, D), F32) + sum(_nbytes(c.shape, c.dtype) for c in consts)
              + sum(_nbytes((tm, w), BF16) for w in out_widths) + _nbytes(vt_shape, BF16))
    temps = _nbytes((tm, 3 * A_WIDTH + 4 * B_HEADS * V7X_LANES), F32)
    return pl.pallas_call(
        functools.partial(_mix_in_kernel, a_scale=HEAD_DIM ** -0.5,
                          m_scale=nope_rope ** -0.5 * math.log2(math.e)),
        grid=(T // tm,),
        in_specs=[row(D), pl.BlockSpec((tm, 1), lambda i: (i, 0))] + [full(c) for c in consts],
        out_specs=[row(w) for w in out_widths]
        + [pl.BlockSpec((None,) + vt_shape, lambda i: (i, 0, 0, 0))],
        out_shape=[jax.ShapeDtypeStruct((T, w), BF16) for w in out_widths]
        + [jax.ShapeDtypeStruct((T // tm,) + vt_shape, BF16)],
        compiler_params=pltpu.CompilerParams(
            dimension_semantics=("parallel",), vmem_limit_bytes=_vmem_limit(blocks, temps)),
        name="mix_in",
    )(x2, pos2, *consts)


def _dilated_kernel(q_ref, kp_ref, kc_ref, vp_ref, vc_ref, o_ref, lse_ref, *, groups, span):
    n = pl.program_id(2)
    lane = lax.broadcasted_iota(I32, (1, V7X_LANES), 1)
    lo = lane < HEAD_DIM
    qi = lax.broadcasted_iota(I32, (BLK, 2 * BLK), 0)
    kj = lax.broadcasted_iota(I32, (BLK, 2 * BLK), 1)
    dist = qi + BLK - kj
    band = (dist >= 0) & (dist <= span)
    first_valid = band & (kj >= jnp.where(n > 0, 0, BLK))
    for hp in range(A_WIDTH // V7X_LANES):
        sl = slice(hp * V7X_LANES, (hp + 1) * V7X_LANES)
        kc = kc_ref[:, sl]
        vc = vc_ref[:, sl]
        for j in range(groups):
            rows = slice(j * BLK, (j + 1) * BLK)
            q2 = q_ref[rows, sl]
            if j == 0:
                k2 = jnp.concatenate([kp_ref[:, sl], kc[:BLK]], axis=0)
                v2 = jnp.concatenate([vp_ref[:, sl], vc[:BLK]], axis=0)
                valid = first_valid
            else:
                k2 = kc[(j - 1) * BLK:(j + 1) * BLK]
                v2 = vc[(j - 1) * BLK:(j + 1) * BLK]
                valid = band
            acc = jnp.zeros((BLK, V7X_LANES), F32)
            lse2 = jnp.zeros((BLK, V7X_LANES), F32)
            for h in range(2):
                sel = lo if h == 0 else ~lo
                qh = jnp.where(sel, q2, jnp.zeros_like(q2))
                vh = jnp.where(sel, v2, jnp.zeros_like(v2))
                s = jnp.where(valid, _dot_nt(qh, k2), NEG_INF)
                m = jnp.max(s, axis=-1, keepdims=True)
                p = jnp.exp(s - m)
                den = jnp.sum(p, axis=-1, keepdims=True)
                acc = acc + _dot(p.astype(BF16), vh) / den
                lse2 = jnp.where(sel, m + jnp.log(den), lse2)
            o_ref[rows, sl] = acc
            lse_ref[rows, sl] = lse2


def _dilated(qa, ka, va, B, S, window, dilation):
    span = window // dilation
    assert span <= BLK
    L = S // dilation
    assert L * dilation == S and L % BLK == 0
    nblk = L // BLK
    groups = math.gcd(nblk, 4)
    W = A_WIDTH
    view = lambda a: a.reshape(B, L, dilation * W)
    cur = pl.BlockSpec((None, groups * BLK, W), lambda b, r, n: (b, n, r))
    prev = pl.BlockSpec((None, BLK, W), lambda b, r, n: (b, jnp.maximum(n * groups - 1, 0), r))
    blocks = 3 * _nbytes((groups * BLK, W), BF16) + 2 * _nbytes((BLK, W), BF16) \
        + 2 * _nbytes((groups * BLK, W), F32)
    o, lse = pl.pallas_call(
        functools.partial(_dilated_kernel, groups=groups, span=span),
        grid=(B, dilation, nblk // groups),
        in_specs=[cur, prev, cur, prev, cur],
        out_specs=[cur, cur],
        out_shape=[jax.ShapeDtypeStruct((B, L, dilation * W), F32)] * 2,
        compiler_params=pltpu.CompilerParams(
            dimension_semantics=("parallel", "parallel", "arbitrary"),
            vmem_limit_bytes=_vmem_limit(blocks, 4 * 1024 * 1024)),
        name=f"dilated_d{dilation}",
    )(view(qa), view(ka), view(ka), view(va), view(va))
    return o.reshape(B * S, W), lse.reshape(B * S, W)


def _mla_kernel(q_ref, k_ref, vt_ref, o_ref, s_ref, *, tq, tk):
    qi = pl.program_id(2)
    qs = [q_ref[:, h * V7X_LANES:(h + 1) * V7X_LANES] for h in range(2)]
    q_start = qi * tq

    def scores_into(slot, ki):
        kb = k_ref[pl.ds(pl.multiple_of(ki * tk, tk), tk), :]
        for h in range(2):
            s_ref[slot, h] = _dot_nt(kb[:, h * V7X_LANES:(h + 1) * V7X_LANES], qs[h])

    def consume(slot, ki, carry, masked):
        ms, ls, accs = carry
        vtb = vt_ref[ki]
        new_ms, new_ls, new_accs = [], [], []
        for h in range(2):
            s = s_ref[slot, h]
            if masked:
                kpos = ki * tk + lax.broadcasted_iota(I32, (tk, tq), 0)
                qpos = q_start + lax.broadcasted_iota(I32, (tk, tq), 1)
                s = jnp.where(kpos <= qpos, s, NEG_INF)
            m_new = jnp.maximum(ms[h], jnp.max(s, axis=0, keepdims=True))
            alpha = jnp.exp2(ms[h] - m_new)
            p = jnp.exp2(s - m_new)
            new_ls.append(alpha * ls[h] + jnp.sum(p, axis=0, keepdims=True))
            new_ms.append(m_new)
            pv = _dot(vtb[h * MLA_V:(h + 1) * MLA_V, :], p.astype(BF16))
            new_accs.append(accs[h] * alpha + pv)
        return tuple(new_ms), tuple(new_ls), tuple(new_accs)

    def pair(j, carry):
        scores_into(1, 2 * j + 1)
        carry = consume(0, 2 * j, carry, False)
        scores_into(0, 2 * j + 2)
        return consume(1, 2 * j + 1, carry, False)

    def even_tail(carry):
        return consume(0, n_full, carry, True)

    def odd_tail(carry):
        scores_into(1, n_full)
        carry = consume(0, n_full - 1, carry, False)
        return consume(1, n_full, carry, True)

    init = ((jnp.full((1, tq), NEG_INF, F32),) * 2, (jnp.zeros((1, tq), F32),) * 2,
            (jnp.zeros((MLA_V, tq), F32),) * 2)
    n_full = q_start // tk
    scores_into(0, 0)
    carry = lax.fori_loop(0, n_full // 2, pair, init)
    _, ls, accs = lax.cond(n_full % 2 == 0, even_tail, odd_tail, carry)
    for h in range(2):
        o_ref[h * MLA_V:(h + 1) * MLA_V, :] = (accs[h] / ls[h]).astype(o_ref.dtype)


def _mla(qm, km, vmt, B, S):
    tk = vmt.shape[-1]
    tq = min(MLA_TQ, S)
    assert tk % tq == 0 and S % tk == 0
    nk = S // tk
    hw = 2 * V7X_LANES
    q3, k3 = (a.reshape(B, S, -1) for a in (qm, km))
    blocks = _nbytes((tq, hw), BF16) + _nbytes((S, hw), BF16) + _nbytes((S, V7X_LANES), BF16) \
        + _nbytes((tq, V7X_LANES), BF16)
    score_scratch = (2, 2, tk, tq)
    temps = _nbytes(score_scratch, F32) + 6 * _nbytes((tk, tq), F32)
    return pl.pallas_call(
        functools.partial(_mla_kernel, tq=tq, tk=tk),
        grid=(B, B_HEADS // 2, S // tq),
        in_specs=[pl.BlockSpec((None, tq, hw), lambda b, h, i: (b, i, h)),
                  pl.BlockSpec((None, S, hw), lambda b, h, i: (b, 0, h)),
                  pl.BlockSpec((nk, None, V7X_LANES, tk), lambda b, h, i: (b, h, 0, 0))],
        out_specs=pl.BlockSpec((None, V7X_LANES, tq), lambda b, h, i: (b, h, i)),
        out_shape=jax.ShapeDtypeStruct((B, B_WIDTH, S), BF16),
        scratch_shapes=[pltpu.VMEM(score_scratch, F32)],
        compiler_params=pltpu.CompilerParams(
            dimension_semantics=("parallel", "parallel", "arbitrary"),
            vmem_limit_bytes=_vmem_limit(blocks, temps)),
        name="mla_attention",
    )(q3, k3, vmt)


def _mix_out_kernel(o1, o2, o3, l1, l2, l3, om_ref, x_ref, wo_ref, gffn_ref, wpq_ref,
                    h_ref, hn_ref, qp_ref):
    lses = [l1[...], l2[...], l3[...]]
    mx = jnp.maximum(jnp.maximum(lses[0], lses[1]), lses[2])
    es = [jnp.exp(l - mx) for l in lses]
    oa = (es[0] * o1[...] + es[1] * o2[...] + es[2] * o3[...]) / (es[0] + es[1] + es[2])
    mixed_m = lax.dot_general(om_ref[...], wo_ref[A_WIDTH:, :], (((0,), (0,)), ((), ())),
                              preferred_element_type=F32)
    h = x_ref[...] + _dot(oa.astype(BF16), wo_ref[:A_WIDTH, :]) + mixed_m
    h_ref[...] = h
    hn = _rms(h, gffn_ref[...]).astype(BF16)
    hn_ref[...] = hn
    qp_ref[...] = _dot(hn, wpq_ref[...])


def _mix_out(os_, lses, omt, x2, w_out, g_ffn, w_pq):
    T, D = x2.shape
    S = omt.shape[-1]
    tm = min(256, S)
    nt = S // tm
    wo = w_out.astype(BF16)
    wpq = w_pq.astype(BF16)
    QW = wpq.shape[1]
    row = lambda w: pl.BlockSpec((tm, w), lambda i: (i, 0))
    full = lambda a: pl.BlockSpec(a.shape, lambda i: (0,) * a.ndim)
    blocks = 6 * _nbytes((tm, A_WIDTH), F32) + _nbytes((tm, B_WIDTH), BF16) \
        + 2 * _nbytes((tm, D), F32) + _nbytes(wo.shape, BF16) + _nbytes(wpq.shape, BF16) \
        + _nbytes((tm, D), BF16) + _nbytes((tm, QW), F32)
    return pl.pallas_call(
        _mix_out_kernel,
        grid=(T // tm,),
        in_specs=[row(A_WIDTH)] * 6
        + [pl.BlockSpec((None, B_WIDTH, tm), lambda i: (i // nt, 0, i % nt)), row(D), full(wo),
           pl.BlockSpec((1, D), lambda i: (0, 0)), full(wpq)],
        out_specs=[row(D), row(D), row(QW)],
        out_shape=[jax.ShapeDtypeStruct((T, D), F32), jax.ShapeDtypeStruct((T, D), BF16),
                   jax.ShapeDtypeStruct((T, QW), F32)],
        compiler_params=pltpu.CompilerParams(
            dimension_semantics=("parallel",),
            vmem_limit_bytes=_vmem_limit(blocks, 4 * _nbytes((tm, QW), F32))),
        name="mix_out",
    )(*os_, *lses, omt, x2, wo, g_ffn.reshape(1, D), wpq)


def _topk_rows(sc, k, order=None, payload=None):
    if order is None:
        order = lax.broadcasted_iota(I32, sc.shape, 0)
    out_rows = lax.broadcasted_iota(I32, (k, sc.shape[1]), 0)
    vals = jnp.zeros((k, sc.shape[1]), F32)
    sel = jnp.zeros((k, sc.shape[1]), I32)
    for i in range(k):
        m = jnp.max(sc, axis=0, keepdims=True)
        first = jnp.min(jnp.where(sc == m, order, _ORDER_NONE), axis=0, keepdims=True)
        hit = order == first
        if payload is None:
            got = first
        else:
            got = jnp.max(jnp.where(hit, payload, -1), axis=0, keepdims=True)
        vals = jnp.where(out_rows == i, m, vals)
        sel = jnp.where(out_rows == i, got, sel)
        sc = jnp.where(hit, -jnp.inf, sc)
    return vals, sel


_ORDER_NONE = 1 << 20


def _candidate_pieces(k):
    pieces = []
    split = 0
    while (split + 1) * (split + 1) <= k:
        split += 1
    for r in range(split):
        n = k // (r + 1)
        for s0 in range(0, n, V7X_SUBLANES):
            pieces.append((True, r, s0, 0, min(V7X_SUBLANES, n - s0)))
    for s in range(split):
        n = k // (s + 1)
        for r0 in range(0, n, V7X_SUBLANES):
            lo, hi = max(split - r0, 0), min(V7X_SUBLANES, n - r0)
            if hi > lo:
                pieces.append((False, s, r0, lo, hi))
    return pieces


def _peer_topk_kernel(qp_ref, keys_ref, i1_ref, i2_ref, gate_ref):
    K = PEER_TOPK
    tt = qp_ref.shape[0]
    pieces = _candidate_pieces(K)
    j8 = lax.broadcasted_iota(I32, (V7X_SUBLANES, tt), 0)
    for h in range(PEER_HEADS):
        vals, idxs = [], []
        for p in range(2):
            c0 = (2 * h + p) * (PEER_QDIM // 2)
            q = qp_ref[:, c0:c0 + PEER_QDIM // 2].astype(BF16)
            sc = _dot_nt(keys_ref[2 * h + p], q)
            v, i = _topk_rows(sc, K)
            vals.append(v)
            idxs.append(i)
        cand, order, pair = [], [], []
        for n, (along_s, fixed, start, lo, hi) in enumerate(pieces):
            a, b = (0, 1) if along_s else (1, 0)
            sl = slice(start, start + V7X_SUBLANES)
            valid = (j8 >= lo) & (j8 < hi)
            cand.append(jnp.where(valid, vals[a][fixed:fixed + 1, :] + vals[b][sl, :], -jnp.inf))
            flat = (fixed * K + start + j8) if along_s else ((start + j8) * K + fixed)
            order.append(jnp.where(valid, flat, _ORDER_NONE - 1 - n * V7X_SUBLANES - j8))
            e1 = idxs[0][fixed:fixed + 1, :] if along_s else idxs[0][sl, :]
            e2 = idxs[1][sl, :] if along_s else idxs[1][fixed:fixed + 1, :]
            pair.append(e1 * PEER_NKEYS + e2)
        best, expert = _topk_rows(jnp.concatenate(cand, axis=0), K,
                                  order=jnp.concatenate(order, axis=0),
                                  payload=jnp.concatenate(pair, axis=0))
        e = jnp.exp(best - jnp.max(best, axis=0, keepdims=True))
        rows = slice(h * K, (h + 1) * K)
        gate_ref[rows, :] = e / jnp.sum(e, axis=0, keepdims=True)
        i1_ref[rows, :] = expert // PEER_NKEYS
        i2_ref[rows, :] = expert % PEER_NKEYS


def _peer_topk(qp, sub_keys):
    T, QW = qp.shape
    tt = min(256, T)
    keys = sub_keys.reshape(PEER_HEADS * 2, PEER_NKEYS, PEER_QDIM // 2).astype(BF16)
    col = pl.BlockSpec((PEER_SEL, tt), lambda i: (0, i))
    blocks = _nbytes((tt, QW), F32) + _nbytes(keys.shape, BF16) + 3 * _nbytes((PEER_SEL, tt), F32)
    return pl.pallas_call(
        _peer_topk_kernel,
        grid=(T // tt,),
        in_specs=[pl.BlockSpec((tt, QW), lambda i: (i, 0)),
                  pl.BlockSpec(keys.shape, lambda i: (0, 0, 0))],
        out_specs=[col, col, col],
        out_shape=[jax.ShapeDtypeStruct((PEER_SEL, T), I32), jax.ShapeDtypeStruct((PEER_SEL, T), I32),
                   jax.ShapeDtypeStruct((PEER_SEL, T), F32)],
        compiler_params=pltpu.CompilerParams(
            dimension_semantics=("parallel",),
            vmem_limit_bytes=_vmem_limit(blocks, 16 * _nbytes((2 * PEER_SEL, tt), F32))),
        name="peer_topk",
    )(qp, keys)


def _peer_act_kernel(hn_ref, u_ref, i1_ref, i2_ref, gate_ref, coef_ref, z_ref, *, chunks):
    g = pl.program_id(1)

    @pl.when(g == 0)
    def _():
        z_ref[...] = jnp.zeros_like(z_ref)

    hn = hn_ref[...]
    i1 = i1_ref[...]
    i2 = i2_ref[...]
    zsel = z_ref[...]
    for k in range(chunks // 2):
        z = _dot_nt(hn, u_ref[2 * k * PEER_NKEYS:(2 * k + 2) * PEER_NKEYS, :])
        for j in range(2):
            picked = jnp.take_along_axis(z[:, j * PEER_NKEYS:(j + 1) * PEER_NKEYS], i2, axis=1)
            zsel = jnp.where(i1 == g * chunks + 2 * k + j, picked, zsel)
    z_ref[...] = zsel

    @pl.when(g == pl.num_programs(1) - 1)
    def _():
        act = 0.5 * zsel * (1.0 + lax.erf(zsel * np.float32(math.sqrt(0.5))))
        coef_ref[...] = act * gate_ref[...]


def _peer_act(hn, u_bf, i1, i2, gate):
    T, D = hn.shape
    E = u_bf.shape[0]
    tm = min(1024, T)
    chunks = 8
    rows = chunks * PEER_NKEYS
    sel = pl.BlockSpec((tm, PEER_SEL), lambda i, g: (i, 0))
    blocks = _nbytes((tm, D), BF16) + _nbytes((rows, D), BF16) + 4 * _nbytes((tm, PEER_SEL), F32)
    return pl.pallas_call(
        functools.partial(_peer_act_kernel, chunks=chunks),
        grid=(T // tm, E // rows),
        in_specs=[pl.BlockSpec((tm, D), lambda i, g: (i, 0)),
                  pl.BlockSpec((rows, D), lambda i, g: (g, 0)), sel, sel, sel],
        out_specs=sel,
        out_shape=jax.ShapeDtypeStruct((T, PEER_SEL), F32),
        scratch_shapes=[pltpu.VMEM((tm, PEER_SEL), F32)],
        compiler_params=pltpu.CompilerParams(
            dimension_semantics=("parallel", "arbitrary"),
            vmem_limit_bytes=_vmem_limit(blocks, 6 * _nbytes((tm, 2 * PEER_NKEYS), F32))),
        name="peer_act",
    )(hn, u_bf, i1, i2, gate)


def _peer_out_kernel(i1_ref, i2_ref, coef_ref, v_ref, h_ref, gfin_ref, out_ref,
                     sx_ref, acc_ref, *, tt, pitch, chunks):
    g = pl.program_id(1)
    NK = PEER_NKEYS

    @pl.when(g == 0)
    def _():
        acc_ref[...] = jnp.zeros_like(acc_ref)
        sub = lax.broadcasted_iota(I32, (NK, PEER_SEL), 0)

        def scatter_token(t, carry):
            i1r = i1_ref[pl.ds(t, 1), :]
            i2r = i2_ref[pl.ds(t, 1), :]
            cr = coef_ref[pl.ds(t, 1), :]
            a_t = jnp.where(sub == i1r, 1.0, 0.0).astype(BF16)
            b_t = jnp.where(sub == i2r, cr, 0.0).astype(BF16)
            s = _dot_nt(a_t, b_t)
            sx_ref[pl.ds(t, NK, stride=pitch), :] = s
            return carry

        lax.fori_loop(0, tt, scatter_token, 0, unroll=SCATTER_UNROLL)

    part = jnp.zeros(acc_ref.shape, F32)
    for cc in range(chunks // 2):
        c0 = g * chunks + 2 * cc
        lhs = jnp.concatenate(
            [sx_ref[pl.ds(pl.multiple_of((c0 + k) * pitch, V7X_SUBLANES), tt), :] for k in range(2)],
            axis=1).astype(BF16)
        part = part + _dot(lhs, v_ref[2 * cc * NK:(2 * cc + 2) * NK, :])
    acc_ref[...] += part

    @pl.when(g == pl.num_programs(1) - 1)
    def _():
        out_ref[...] = _rms(h_ref[...] + acc_ref[...], gfin_ref[...])


def _peer_out(i1, i2, coef, v_bf, h, g_final):
    T, D = h.shape
    E = v_bf.shape[0]
    tt = min(256, T)
    chunks = 16
    pitch = tt + V7X_SUBLANES
    rows = chunks * PEER_NKEYS
    sel = pl.BlockSpec((tt, PEER_SEL), lambda i, g: (i, 0))
    tok = pl.BlockSpec((tt, D), lambda i, g: (i, 0))
    blocks = 3 * _nbytes((tt, PEER_SEL), F32) + _nbytes((rows, D), BF16) + 2 * _nbytes((tt, D), F32)
    scratch = _nbytes((PEER_NKEYS * pitch, PEER_NKEYS), F32) + _nbytes((tt, D), F32)
    return pl.pallas_call(
        functools.partial(_peer_out_kernel, tt=tt, pitch=pitch, chunks=chunks),
        grid=(T // tt, E // rows),
        in_specs=[sel, sel, sel, pl.BlockSpec((rows, D), lambda i, g: (g, 0)), tok,
                  pl.BlockSpec((1, D), lambda i, g: (0, 0))],
        out_specs=tok,
        out_shape=jax.ShapeDtypeStruct((T, D), F32),
        scratch_shapes=[pltpu.VMEM((PEER_NKEYS * pitch, PEER_NKEYS), F32),
                        pltpu.VMEM((tt, D), F32)],
        compiler_params=pltpu.CompilerParams(
            dimension_semantics=("parallel", "arbitrary"),
            vmem_limit_bytes=_vmem_limit(blocks, scratch + 4 * _nbytes((tt, D), F32))),
        name="peer_out",
    )(i1, i2, coef, v_bf, h, g_final.reshape(1, D))


def kernel(x, positions, g_mix, w_in, g_cq, w_uq, g_ckv, w_ukv, w_out, g_ffn,
           w_pq, sub_keys, u_emb, v_emb, g_final):
    B, S, D = x.shape
    T = B * S
    assert g_mix.shape[0] == 1, "single-layer configuration"
    x2 = x.reshape(T, D)
    pos2 = positions.reshape(T, 1)

    qa, ka, va, qm, km, vmt = _mix_in(x2, pos2, g_mix[0], w_in[0], g_cq[0], w_uq[0],
                                      g_ckv[0], w_ukv[0], tm=min(MLA_TK, S))
    outs, lses = zip(*[_dilated(qa, ka, va, B, S, w, d) for w, d in DIL_PATTERNS])
    omt = _mla(qm, km, vmt, B, S)
    h, hn, qp = _mix_out(outs, lses, omt, x2, w_out[0], g_ffn[0], w_pq[0])
    i1, i2, gate = _peer_topk(qp, sub_keys[0])
    i1, i2, gate = i1.T, i2.T, gate.T
    coef = _peer_act(hn, u_emb[0].astype(BF16), i1, i2, gate)
    out = _peer_out(i1, i2, coef, v_emb[0].astype(BF16), h, g_final)
    return out.reshape(B, S, D)
```

```python
import functools
import math

import jax
import jax.numpy as jnp
import numpy as np
from jax import lax
from jax.experimental import pallas as pl
from jax.experimental.pallas import tpu as pltpu

F32 = jnp.float32
BF16 = jnp.bfloat16
I32 = jnp.int32

ROPE_THETA = 500000.0
NORM_EPS = 1e-6
NEG_INF = -1e30
BLK = 128
A_HEADS = 8
HEAD_DIM = 64
A_ROT_DIM = HEAD_DIM // 4
DIL_PATTERNS = ((128, 1), (512, 4), (2048, 16))
A_WIDTH = A_HEADS * HEAD_DIM
B_HEADS = 8
MLA_Q_RANK = 256
MLA_KV_RANK = 128
MLA_NOPE = 64
MLA_ROPE = 32
MLA_V = 64
B_WIDTH = B_HEADS * MLA_V
PEER_HEADS = 8
PEER_NKEYS = 128
PEER_QDIM = 256
PEER_TOPK = 16
PEER_SEL = PEER_HEADS * PEER_TOPK
SCATTER_UNROLL = 16
MLA_TQ = 512
MLA_TK = 512

V7X_LANES = 128
V7X_SUBLANES = 8
V7X_VMEM_BYTES = 64 * 1024 * 1024
V7X_VMEM_REQUEST_CAP = 56 * 1024 * 1024


def _vmem_limit(pipelined_bytes, resident_bytes=0):
    est = 2 * int(pipelined_bytes) + int(resident_bytes)
    return int(min(max(est, 16 * 1024 * 1024), V7X_VMEM_REQUEST_CAP))


def _nbytes(shape, dtype):
    return int(np.prod(shape)) * jnp.dtype(dtype).itemsize


def _dot(a, b):
    return jnp.dot(a, b, preferred_element_type=F32)


def _dot_nt(a, b):
    return lax.dot_general(a, b, (((1,), (1,)), ((), ())), preferred_element_type=F32)


def _rms(x, g):
    ms = jnp.mean(x * x, axis=-1, keepdims=True)
    return x * lax.rsqrt(ms + NORM_EPS) * g


def _rope_tables(pos_f32, inv_row, lo_start, half, period):
    ang = pos_f32 * inv_row
    cos, sin = jnp.cos(ang), jnp.sin(ang)
    lane = lax.broadcasted_iota(I32, (1, V7X_LANES), 1) & (period - 1)
    is_lo = (lane >= lo_start) & (lane < lo_start + half)
    is_hi = (lane >= lo_start + half) & (lane < lo_start + 2 * half)
    c = jnp.where(is_lo | is_hi, cos, 1.0)
    s_lo = jnp.where(is_lo, -sin, 0.0)
    s_hi = jnp.where(is_hi, sin, 0.0)
    return c, s_lo, s_hi


def _apply_rope(x, tables, half):
    c, s_lo, s_hi = tables
    return (x * c + pltpu.roll(x, V7X_LANES - half, 1) * s_lo
            + pltpu.roll(x, half, 1) * s_hi)


def _mix_in_kernel(x_ref, pos_ref, gmix_ref, wa_ref, wcq_ref, wckv_ref, wkr_ref,
                   gcq_ref, wuq_ref, gckv_ref, wuk_ref, wuvt_ref, inva_ref, invm_ref,
                   qa_ref, ka_ref, va_ref, qm_ref, km_ref, vmt_ref, *, a_scale, m_scale):
    hn = _rms(x_ref[...], gmix_ref[...]).astype(BF16)
    pos = pos_ref[...].astype(F32)
    tab_a = _rope_tables(pos, inva_ref[...], 0, A_ROT_DIM // 2, HEAD_DIM)
    tab_m = _rope_tables(pos, invm_ref[...], MLA_NOPE, MLA_ROPE // 2, V7X_LANES)

    proj = _dot(hn, wa_ref[...])
    for c in range(A_WIDTH // V7X_LANES):
        sl = slice(c * V7X_LANES, (c + 1) * V7X_LANES)
        q = proj[:, sl]
        k = proj[:, A_WIDTH + c * V7X_LANES:A_WIDTH + (c + 1) * V7X_LANES]
        qa_ref[c] = _apply_rope(q, tab_a, A_ROT_DIM // 2) * a_scale
        ka_ref[c] = _apply_rope(k, tab_a, A_ROT_DIM // 2)
        va_ref[c] = proj[:, 2 * A_WIDTH + c * V7X_LANES:2 * A_WIDTH + (c + 1) * V7X_LANES]

    cq = _rms(_dot(hn, wcq_ref[...]), gcq_ref[...]).astype(BF16)
    qm = _dot(cq, wuq_ref[...])
    ckv = _rms(_dot(hn, wckv_ref[...]), gckv_ref[...]).astype(BF16)
    kn = _dot(ckv, wuk_ref[...])
    kr = _apply_rope(_dot(hn, wkr_ref[...]), tab_m, MLA_ROPE // 2)
    for h in range(B_HEADS):
        sl = slice(h * V7X_LANES, (h + 1) * V7X_LANES)
        qm_ref[:, sl] = (_apply_rope(qm[:, sl], tab_m, MLA_ROPE // 2) * m_scale).astype(BF16)
        km_ref[:, sl] = (kn[:, sl] + kr).astype(BF16)
    vmt_ref[...] = _dot_nt(wuvt_ref[...], ckv).astype(BF16).reshape(vmt_ref.shape)


def _mix_in(x2, pos2, g_mix, w_in, g_cq, w_uq, g_ckv, w_ukv, tm):
    T, D = x2.shape
    nope_rope = MLA_NOPE + MLA_ROPE
    wa = w_in[:, :3 * A_WIDTH].astype(BF16)
    o0 = 3 * A_WIDTH
    wcq = w_in[:, o0:o0 + MLA_Q_RANK].astype(BF16)
    wckv = w_in[:, o0 + MLA_Q_RANK:o0 + MLA_Q_RANK + MLA_KV_RANK].astype(BF16)
    wkr_raw = w_in[:, o0 + MLA_Q_RANK + MLA_KV_RANK:]
    wkr = jnp.pad(wkr_raw, ((0, 0), (MLA_NOPE, V7X_LANES - nope_rope))).astype(BF16)
    wuq = jnp.pad(w_uq.reshape(MLA_Q_RANK, B_HEADS, nope_rope),
                  ((0, 0), (0, 0), (0, V7X_LANES - nope_rope))
                  ).reshape(MLA_Q_RANK, B_HEADS * V7X_LANES).astype(BF16)
    wukv = w_ukv.reshape(MLA_KV_RANK, B_HEADS, MLA_NOPE + MLA_V)
    wuk = jnp.pad(wukv[:, :, :MLA_NOPE], ((0, 0), (0, 0), (0, V7X_LANES - MLA_NOPE))
                  ).reshape(MLA_KV_RANK, B_HEADS * V7X_LANES).astype(BF16)
    wuvt = wukv[:, :, MLA_NOPE:].reshape(MLA_KV_RANK, B_WIDTH).T.astype(BF16)
    lane = jnp.arange(V7X_LANES)
    ha, hm = A_ROT_DIM // 2, MLA_ROPE // 2
    inv_a = (ROPE_THETA ** (-jnp.arange(ha, dtype=F32) / ha))[lane % ha].reshape(1, V7X_LANES)
    inv_m = (ROPE_THETA ** (-jnp.arange(hm, dtype=F32) / hm))[lane % hm].reshape(1, V7X_LANES)

    full = lambda a: pl.BlockSpec(a.shape, lambda i: (0,) * a.ndim)
    row = lambda w: pl.BlockSpec((tm, w), lambda i: (i, 0))
    consts = (g_mix.reshape(1, D), wa, wcq, wckv, wkr, g_cq.reshape(1, -1), wuq,
              g_ckv.reshape(1, -1), wuk, wuvt, inv_a, inv_m)
    a_shape = (A_WIDTH // V7X_LANES, tm, V7X_LANES)
    m_width = B_HEADS * V7X_LANES
    vt_shape = (B_HEADS // 2, 2 * MLA_V, tm)
    blocks = (_nbytes((tm, D), F32) + sum(_nbytes(c.shape, c.dtype) for c in consts)
              + 3 * _nbytes(a_shape, F32) + 2 * _nbytes((tm, m_width), BF16)
              + _nbytes(vt_shape, BF16))
    temps = _nbytes((tm, 3 * A_WIDTH + 4 * B_HEADS * V7X_LANES), F32)
    a_spec = pl.BlockSpec(a_shape, lambda i: (0, i, 0))
    a_out = jax.ShapeDtypeStruct((a_shape[0], T, V7X_LANES), F32)
    return pl.pallas_call(
        functools.partial(_mix_in_kernel, a_scale=HEAD_DIM ** -0.5,
                          m_scale=nope_rope ** -0.5 * math.log2(math.e)),
        grid=(T // tm,),
        in_specs=[row(D), pl.BlockSpec((tm, 1), lambda i: (i, 0))] + [full(c) for c in consts],
        out_specs=[a_spec] * 3 + [row(m_width)] * 2
        + [pl.BlockSpec((None,) + vt_shape, lambda i: (i, 0, 0, 0))],
        out_shape=[a_out] * 3 + [jax.ShapeDtypeStruct((T, m_width), BF16)] * 2
        + [jax.ShapeDtypeStruct((T // tm,) + vt_shape, BF16)],
        compiler_params=pltpu.CompilerParams(
            dimension_semantics=("parallel",), vmem_limit_bytes=_vmem_limit(blocks, temps)),
        name="mix_in",
    )(x2, pos2, *consts)


def _dilated_kernel(q_ref, kp_ref, kc_ref, vp_ref, vc_ref, o_ref, *scratch, patterns, rows):
    n = pl.program_id(2)
    o_scr, l_scr = scratch[:len(patterns)], scratch[len(patterns):]
    lane = lax.broadcasted_iota(I32, (1, V7X_LANES), 1)
    lo = lane < HEAD_DIM
    qi = lax.broadcasted_iota(I32, (BLK, 2 * BLK), 0)
    kj = lax.broadcasted_iota(I32, (BLK, 2 * BLK), 1)
    dist = qi + BLK - kj

    for pi, (span, dil) in enumerate(patterns):
        band = (dist >= 0) & (dist <= span)
        first_valid = band & (kj >= jnp.where(n > 0, 0, BLK))
        step = BLK * dil

        def take(ref, start, dil=dil):
            idx = pl.ds(start, BLK, stride=dil) if dil > 1 else pl.ds(start, BLK)
            return ref[idx, :]

        for r in range(dil):
            for j in range(rows // step):
                start = j * step + r
                q2 = take(q_ref, start).astype(BF16)
                if j == 0:
                    kprev, vprev = take(kp_ref, rows - step + r), take(vp_ref, rows - step + r)
                    valid = first_valid
                else:
                    kprev, vprev = take(kc_ref, start - step), take(vc_ref, start - step)
                    valid = band
                k2 = jnp.concatenate([kprev, take(kc_ref, start)], axis=0).astype(BF16)
                v2 = jnp.concatenate([vprev, take(vc_ref, start)], axis=0).astype(BF16)
                acc = jnp.zeros((BLK, V7X_LANES), F32)
                lse2 = jnp.zeros((BLK, V7X_LANES), F32)
                for h in range(2):
                    sel = lo if h == 0 else ~lo
                    qh = jnp.where(sel, q2, jnp.zeros_like(q2))
                    vh = jnp.where(sel, v2, jnp.zeros_like(v2))
                    s = jnp.where(valid, _dot_nt(qh, k2), NEG_INF)
                    m = jnp.max(s, axis=-1, keepdims=True)
                    p = jnp.exp(s - m)
                    den = jnp.sum(p, axis=-1, keepdims=True)
                    acc = acc + _dot(p.astype(BF16), vh) / den
                    lse2 = jnp.where(sel, m + jnp.log(den), lse2)
                idx = pl.ds(start, BLK, stride=dil) if dil > 1 else pl.ds(start, BLK)
                o_scr[pi][idx, :] = acc
                l_scr[pi][idx, :] = lse2

    lses = [l[...] for l in l_scr]
    mx = functools.reduce(jnp.maximum, lses)
    es = [jnp.exp(l - mx) for l in lses]
    num = sum(e * o[...] for e, o in zip(es, o_scr))
    o_ref[...] = (num / sum(es)).astype(o_ref.dtype)


def _dilated(qa, ka, va, B, S):
    patterns = tuple((w // d, d) for w, d in DIL_PATTERNS)
    rows = BLK * max(d for _, d in patterns)
    assert all(span <= BLK for span, _ in patterns) and S % rows == 0
    nb = S // rows
    npairs = qa.shape[0]
    cur = pl.BlockSpec((None, rows, V7X_LANES), lambda b, h, n: (h, b * nb + n, 0))
    prev = pl.BlockSpec((None, rows, V7X_LANES),
                        lambda b, h, n: (h, b * nb + jnp.maximum(n - 1, 0), 0))
    blocks = 5 * _nbytes((rows, V7X_LANES), F32) + _nbytes((rows, V7X_LANES), BF16)
    scratch = 2 * len(patterns) * _nbytes((rows, V7X_LANES), F32)
    return pl.pallas_call(
        functools.partial(_dilated_kernel, patterns=patterns, rows=rows),
        grid=(B, npairs, nb),
        in_specs=[cur, prev, cur, prev, cur],
        out_specs=cur,
        out_shape=jax.ShapeDtypeStruct(qa.shape, BF16),
        scratch_shapes=[pltpu.VMEM((rows, V7X_LANES), F32)] * (2 * len(patterns)),
        compiler_params=pltpu.CompilerParams(
            dimension_semantics=("parallel", "parallel", "arbitrary"),
            vmem_limit_bytes=_vmem_limit(blocks, scratch + 4 * 1024 * 1024)),
        name="dilated_attention",
    )(qa, ka, ka, va, va)


def _mla_kernel(q_ref, k_ref, vt_ref, o_ref, s_ref, *, tq, tk):
    qi = pl.program_id(2)
    qs = [q_ref[:, h * V7X_LANES:(h + 1) * V7X_LANES] for h in range(2)]
    q_start = qi * tq

    def scores_into(slot, ki):
        kb = k_ref[pl.ds(pl.multiple_of(ki * tk, tk), tk), :]
        for h in range(2):
            s_ref[slot, h] = _dot_nt(kb[:, h * V7X_LANES:(h + 1) * V7X_LANES], qs[h])

    def consume(slot, ki, carry, masked):
        ms, ls, accs = carry
        vtb = vt_ref[ki]
        new_ms, new_ls, new_accs = [], [], []
        for h in range(2):
            s = s_ref[slot, h]
            if masked:
                kpos = ki * tk + lax.broadcasted_iota(I32, (tk, tq), 0)
                qpos = q_start + lax.broadcasted_iota(I32, (tk, tq), 1)
                s = jnp.where(kpos <= qpos, s, NEG_INF)
            m_new = jnp.maximum(ms[h], jnp.max(s, axis=0, keepdims=True))
            alpha = jnp.exp2(ms[h] - m_new)
            p = jnp.exp2(s - m_new)
            new_ls.append(alpha * ls[h] + jnp.sum(p, axis=0, keepdims=True))
            new_ms.append(m_new)
            pv = _dot(vtb[h * MLA_V:(h + 1) * MLA_V, :], p.astype(BF16))
            new_accs.append(accs[h] * alpha + pv)
        return tuple(new_ms), tuple(new_ls), tuple(new_accs)

    def pair(j, carry):
        scores_into(1, 2 * j + 1)
        carry = consume(0, 2 * j, carry, False)
        scores_into(0, 2 * j + 2)
        return consume(1, 2 * j + 1, carry, False)

    def even_tail(carry):
        return consume(0, n_full, carry, True)

    def odd_tail(carry):
        scores_into(1, n_full)
        carry = consume(0, n_full - 1, carry, False)
        return consume(1, n_full, carry, True)

    init = ((jnp.full((1, tq), NEG_INF, F32),) * 2, (jnp.zeros((1, tq), F32),) * 2,
            (jnp.zeros((MLA_V, tq), F32),) * 2)
    n_full = q_start // tk
    scores_into(0, 0)
    carry = lax.fori_loop(0, n_full // 2, pair, init)
    _, ls, accs = lax.cond(n_full % 2 == 0, even_tail, odd_tail, carry)
    for h in range(2):
        o_ref[h * MLA_V:(h + 1) * MLA_V, :] = (accs[h] / ls[h]).astype(o_ref.dtype)


def _mla(qm, km, vmt, B, S):
    tk = vmt.shape[-1]
    tq = min(MLA_TQ, S)
    assert tk % tq == 0 and S % tk == 0
    nk = S // tk
    hw = 2 * V7X_LANES
    q3, k3 = (a.reshape(B, S, -1) for a in (qm, km))
    blocks = _nbytes((tq, hw), BF16) + _nbytes((S, hw), BF16) + _nbytes((S, V7X_LANES), BF16) \
        + _nbytes((tq, V7X_LANES), BF16)
    score_scratch = (2, 2, tk, tq)
    temps = _nbytes(score_scratch, F32) + 6 * _nbytes((tk, tq), F32)
    return pl.pallas_call(
        functools.partial(_mla_kernel, tq=tq, tk=tk),
        grid=(B, B_HEADS // 2, S // tq),
        in_specs=[pl.BlockSpec((None, tq, hw), lambda b, h, i: (b, i, h)),
                  pl.BlockSpec((None, S, hw), lambda b, h, i: (b, 0, h)),
                  pl.BlockSpec((nk, None, V7X_LANES, tk), lambda b, h, i: (b, h, 0, 0))],
        out_specs=pl.BlockSpec((None, V7X_LANES, tq), lambda b, h, i: (b, h, i)),
        out_shape=jax.ShapeDtypeStruct((B, B_WIDTH, S), BF16),
        scratch_shapes=[pltpu.VMEM(score_scratch, F32)],
        compiler_params=pltpu.CompilerParams(
            dimension_semantics=("parallel", "parallel", "arbitrary"),
            vmem_limit_bytes=_vmem_limit(blocks, temps)),
        name="mla_attention",
    )(q3, k3, vmt)


def _mix_out_kernel(oa_ref, om_ref, x_ref, wo_ref, gffn_ref, wpq_ref, h_ref, hn_ref, qp_ref):
    oa = jnp.concatenate([oa_ref[c] for c in range(oa_ref.shape[0])], axis=1)
    mixed_m = lax.dot_general(om_ref[...], wo_ref[A_WIDTH:, :], (((0,), (0,)), ((), ())),
                              preferred_element_type=F32)
    h = x_ref[...] + _dot(oa, wo_ref[:A_WIDTH, :]) + mixed_m
    h_ref[...] = h
    hn = _rms(h, gffn_ref[...]).astype(BF16)
    hn_ref[...] = hn
    qp_ref[...] = _dot(hn, wpq_ref[...])


def _mix_out(oa4, omt, x2, w_out, g_ffn, w_pq):
    T, D = x2.shape
    S = omt.shape[-1]
    tm = min(256, S)
    nt = S // tm
    wo = w_out.astype(BF16)
    wpq = w_pq.astype(BF16)
    QW = wpq.shape[1]
    row = lambda w: pl.BlockSpec((tm, w), lambda i: (i, 0))
    full = lambda a: pl.BlockSpec(a.shape, lambda i: (0,) * a.ndim)
    blocks = _nbytes((tm, A_WIDTH), BF16) + _nbytes((tm, B_WIDTH), BF16) \
        + 2 * _nbytes((tm, D), F32) + _nbytes(wo.shape, BF16) + _nbytes(wpq.shape, BF16) \
        + _nbytes((tm, D), BF16) + _nbytes((tm, QW), F32)
    return pl.pallas_call(
        _mix_out_kernel,
        grid=(T // tm,),
        in_specs=[pl.BlockSpec((oa4.shape[0], tm, V7X_LANES), lambda i: (0, i, 0)),
                  pl.BlockSpec((None, B_WIDTH, tm), lambda i: (i // nt, 0, i % nt)), row(D), full(wo),
                  pl.BlockSpec((1, D), lambda i: (0, 0)), full(wpq)],
        out_specs=[row(D), row(D), row(QW)],
        out_shape=[jax.ShapeDtypeStruct((T, D), F32), jax.ShapeDtypeStruct((T, D), BF16),
                   jax.ShapeDtypeStruct((T, QW), F32)],
        compiler_params=pltpu.CompilerParams(
            dimension_semantics=("parallel",),
            vmem_limit_bytes=_vmem_limit(blocks, 4 * _nbytes((tm, QW), F32))),
        name="mix_out",
    )(oa4, omt, x2, wo, g_ffn.reshape(1, D), wpq)


def _topk_rows(sc, k, order=None, payload=None):
    if order is None:
        order = lax.broadcasted_iota(I32, sc.shape, 0)
    out_rows = lax.broadcasted_iota(I32, (k, sc.shape[1]), 0)
    vals = jnp.zeros((k, sc.shape[1]), F32)
    sel = jnp.zeros((k, sc.shape[1]), I32)
    for i in range(k):
        m = jnp.max(sc, axis=0, keepdims=True)
        first = jnp.min(jnp.where(sc == m, order, _ORDER_NONE), axis=0, keepdims=True)
        hit = order == first
        if payload is None:
            got = first
        else:
            got = jnp.max(jnp.where(hit, payload, -1), axis=0, keepdims=True)
        vals = jnp.where(out_rows == i, m, vals)
        sel = jnp.where(out_rows == i, got, sel)
        sc = jnp.where(hit, -jnp.inf, sc)
    return vals, sel


_ORDER_NONE = 1 << 20


def _candidate_pieces(k):
    pieces = []
    split = 0
    while (split + 1) * (split + 1) <= k:
        split += 1
    for r in range(split):
        n = k // (r + 1)
        for s0 in range(0, n, V7X_SUBLANES):
            pieces.append((True, r, s0, 0, min(V7X_SUBLANES, n - s0)))
    for s in range(split):
        n = k // (s + 1)
        for r0 in range(0, n, V7X_SUBLANES):
            lo, hi = max(split - r0, 0), min(V7X_SUBLANES, n - r0)
            if hi > lo:
                pieces.append((False, s, r0, lo, hi))
    return pieces


def _peer_topk_kernel(qp_ref, keys_ref, i1_ref, i2_ref, gate_ref):
    K = PEER_TOPK
    tt = qp_ref.shape[0]
    pieces = _candidate_pieces(K)
    j8 = lax.broadcasted_iota(I32, (V7X_SUBLANES, tt), 0)
    for h in range(PEER_HEADS):
        vals, idxs = [], []
        for p in range(2):
            c0 = (2 * h + p) * (PEER_QDIM // 2)
            q = qp_ref[:, c0:c0 + PEER_QDIM // 2].astype(BF16)
            sc = _dot_nt(keys_ref[2 * h + p], q)
            v, i = _topk_rows(sc, K)
            vals.append(v)
            idxs.append(i)
        cand, order, pair = [], [], []
        for n, (along_s, fixed, start, lo, hi) in enumerate(pieces):
            a, b = (0, 1) if along_s else (1, 0)
            sl = slice(start, start + V7X_SUBLANES)
            valid = (j8 >= lo) & (j8 < hi)
            cand.append(jnp.where(valid, vals[a][fixed:fixed + 1, :] + vals[b][sl, :], -jnp.inf))
            flat = (fixed * K + start + j8) if along_s else ((start + j8) * K + fixed)
            order.append(jnp.where(valid, flat, _ORDER_NONE - 1 - n * V7X_SUBLANES - j8))
            e1 = idxs[0][fixed:fixed + 1, :] if along_s else idxs[0][sl, :]
            e2 = idxs[1][sl, :] if along_s else idxs[1][fixed:fixed + 1, :]
            pair.append(e1 * PEER_NKEYS + e2)
        best, expert = _topk_rows(jnp.concatenate(cand, axis=0), K,
                                  order=jnp.concatenate(order, axis=0),
                                  payload=jnp.concatenate(pair, axis=0))
        e = jnp.exp(best - jnp.max(best, axis=0, keepdims=True))
        rows = slice(h * K, (h + 1) * K)
        gate_ref[rows, :] = e / jnp.sum(e, axis=0, keepdims=True)
        i1_ref[rows, :] = expert // PEER_NKEYS
        i2_ref[rows, :] = expert % PEER_NKEYS


def _peer_topk(qp, sub_keys):
    T, QW = qp.shape
    tt = min(256, T)
    keys = sub_keys.reshape(PEER_HEADS * 2, PEER_NKEYS, PEER_QDIM // 2).astype(BF16)
    col = pl.BlockSpec((PEER_SEL, tt), lambda i: (0, i))
    blocks = _nbytes((tt, QW), F32) + _nbytes(keys.shape, BF16) + 3 * _nbytes((PEER_SEL, tt), F32)
    return pl.pallas_call(
        _peer_topk_kernel,
        grid=(T // tt,),
        in_specs=[pl.BlockSpec((tt, QW), lambda i: (i, 0)),
                  pl.BlockSpec(keys.shape, lambda i: (0, 0, 0))],
        out_specs=[col, col, col],
        out_shape=[jax.ShapeDtypeStruct((PEER_SEL, T), I32), jax.ShapeDtypeStruct((PEER_SEL, T), I32),
                   jax.ShapeDtypeStruct((PEER_SEL, T), F32)],
        compiler_params=pltpu.CompilerParams(
            dimension_semantics=("parallel",),
            vmem_limit_bytes=_vmem_limit(blocks, 16 * _nbytes((2 * PEER_SEL, tt), F32))),
        name="peer_topk",
    )(qp, keys)


def _peer_act_kernel(hn_ref, u_ref, i1_ref, i2_ref, gate_ref, coef_ref, z_ref, *, chunks):
    g = pl.program_id(1)

    @pl.when(g == 0)
    def _():
        z_ref[...] = jnp.zeros_like(z_ref)

    hn = hn_ref[...]
    i1 = i1_ref[...]
    i2 = i2_ref[...]
    zsel = z_ref[...]
    for k in range(chunks // 2):
        z = _dot_nt(hn, u_ref[2 * k * PEER_NKEYS:(2 * k + 2) * PEER_NKEYS, :])
        for j in range(2):
            picked = jnp.take_along_axis(z[:, j * PEER_NKEYS:(j + 1) * PEER_NKEYS], i2, axis=1)
            zsel = jnp.where(i1 == g * chunks + 2 * k + j, picked, zsel)
    z_ref[...] = zsel

    @pl.when(g == pl.num_programs(1) - 1)
    def _():
        act = 0.5 * zsel * (1.0 + lax.erf(zsel * np.float32(math.sqrt(0.5))))
        coef_ref[...] = act * gate_ref[...]


def _peer_act(hn, u_bf, i1, i2, gate):
    T, D = hn.shape
    E = u_bf.shape[0]
    tm = min(1024, T)
    chunks = 8
    rows = chunks * PEER_NKEYS
    sel = pl.BlockSpec((tm, PEER_SEL), lambda i, g: (i, 0))
    blocks = _nbytes((tm, D), BF16) + _nbytes((rows, D), BF16) + 4 * _nbytes((tm, PEER_SEL), F32)
    return pl.pallas_call(
        functools.partial(_peer_act_kernel, chunks=chunks),
        grid=(T // tm, E // rows),
        in_specs=[pl.BlockSpec((tm, D), lambda i, g: (i, 0)),
                  pl.BlockSpec((rows, D), lambda i, g: (g, 0)), sel, sel, sel],
        out_specs=sel,
        out_shape=jax.ShapeDtypeStruct((T, PEER_SEL), F32),
        scratch_shapes=[pltpu.VMEM((tm, PEER_SEL), F32)],
        compiler_params=pltpu.CompilerParams(
            dimension_semantics=("parallel", "arbitrary"),
            vmem_limit_bytes=_vmem_limit(blocks, 6 * _nbytes((tm, 2 * PEER_NKEYS), F32))),
        name="peer_act",
    )(hn, u_bf, i1, i2, gate)


def _peer_out_kernel(i1_ref, i2_ref, coef_ref, v_ref, h_ref, gfin_ref, out_ref,
                     sx_ref, acc_ref, *, tt, pitch, chunks):
    NK = PEER_NKEYS
    sub = lax.broadcasted_iota(I32, (NK, PEER_SEL), 0)

    def scatter_token(t, carry):
        i1r = i1_ref[pl.ds(t, 1), :]
        i2r = i2_ref[pl.ds(t, 1), :]
        cr = coef_ref[pl.ds(t, 1), :]
        a_t = jnp.where(sub == i1r, 1.0, 0.0).astype(BF16)
        b_t = jnp.where(sub == i2r, cr, 0.0).astype(BF16)
        s = _dot_nt(a_t, b_t)
        sx_ref[pl.ds(t, NK, stride=pitch), :] = s
        return carry

    lax.fori_loop(0, tt, scatter_token, 0, unroll=SCATTER_UNROLL)

    acc_ref[...] = jnp.zeros_like(acc_ref)

    def chunk_group(g, carry):
        part = jnp.zeros(acc_ref.shape, F32)
        for cc in range(chunks // 2):
            c0 = g * chunks + 2 * cc
            lhs = jnp.concatenate(
                [sx_ref[pl.ds(pl.multiple_of((c0 + k) * pitch, V7X_SUBLANES), tt), :]
                 for k in range(2)], axis=1).astype(BF16)
            rhs = v_ref[pl.ds(pl.multiple_of(c0 * NK, 2 * NK), 2 * NK), :]
            part = part + _dot(lhs, rhs)
        acc_ref[...] += part
        return carry

    lax.fori_loop(0, v_ref.shape[0] // (chunks * NK), chunk_group, 0)
    out_ref[...] = _rms(h_ref[...] + acc_ref[...], gfin_ref[...])


def _peer_out(i1, i2, coef, v_bf, h, g_final):
    T, D = h.shape
    E = v_bf.shape[0]
    tt = min(256, T)
    chunks = 16
    pitch = tt + V7X_SUBLANES
    sel = pl.BlockSpec((tt, PEER_SEL), lambda i: (i, 0))
    tok = pl.BlockSpec((tt, D), lambda i: (i, 0))
    blocks = 3 * _nbytes((tt, PEER_SEL), F32) + 2 * _nbytes((tt, D), F32)
    scratch = _nbytes((PEER_NKEYS * pitch, PEER_NKEYS), F32) + _nbytes((tt, D), F32)
    resident = _nbytes((E, D), BF16)
    return pl.pallas_call(
        functools.partial(_peer_out_kernel, tt=tt, pitch=pitch, chunks=chunks),
        grid=(T // tt,),
        in_specs=[sel, sel, sel,
                  pl.BlockSpec((E, D), lambda i: (0, 0), pipeline_mode=pl.Buffered(1)), tok,
                  pl.BlockSpec((1, D), lambda i: (0, 0))],
        out_specs=tok,
        out_shape=jax.ShapeDtypeStruct((T, D), F32),
        scratch_shapes=[pltpu.VMEM((PEER_NKEYS * pitch, PEER_NKEYS), F32),
                        pltpu.VMEM((tt, D), F32)],
        compiler_params=pltpu.CompilerParams(
            dimension_semantics=("parallel",),
            vmem_limit_bytes=_vmem_limit(blocks, resident + scratch + 2 * _nbytes((tt, D), F32))),
        name="peer_out",
    )(i1, i2, coef, v_bf, h, g_final.reshape(1, D))


def kernel(x, positions, g_mix, w_in, g_cq, w_uq, g_ckv, w_ukv, w_out, g_ffn,
           w_pq, sub_keys, u_emb, v_emb, g_final):
    B, S, D = x.shape
    T = B * S
    assert g_mix.shape[0] == 1, "single-layer configuration"
    x2 = x.reshape(T, D)
    pos2 = positions.reshape(T, 1)

    qa, ka, va, qm, km, vmt = _mix_in(x2, pos2, g_mix[0], w_in[0], g_cq[0], w_uq[0],
                                      g_ckv[0], w_ukv[0], tm=min(MLA_TK, S))
    oa4 = _dilated(qa, ka, va, B, S)
    omt = _mla(qm, km, vmt, B, S)
    h, hn, qp = _mix_out(oa4, omt, x2, w_out[0], g_ffn[0], w_pq[0])
    i1, i2, gate = _peer_topk(qp, sub_keys[0])
    i1, i2, gate = i1.T, i2.T, gate.T
    coef = _peer_act(hn, u_emb[0].astype(BF16), i1, i2, gate)
    out = _peer_out(i1, i2, coef, v_emb[0].astype(BF16), h, g_final)
    return out.reshape(B, S, D)
```

```python
import functools
import math

import jax
import jax.numpy as jnp
import numpy as np
from jax import lax
from jax.experimental import pallas as pl
from jax.experimental.pallas import tpu as pltpu

F32 = jnp.float32
BF16 = jnp.bfloat16
I32 = jnp.int32

ROPE_THETA = 500000.0
NORM_EPS = 1e-6
NEG_INF = -1e30
BLK = 128
A_HEADS = 8
HEAD_DIM = 64
A_ROT_DIM = HEAD_DIM // 4
DIL_PATTERNS = ((128, 1), (512, 4), (2048, 16))
A_WIDTH = A_HEADS * HEAD_DIM
B_HEADS = 8
MLA_Q_RANK = 256
MLA_KV_RANK = 128
MLA_NOPE = 64
MLA_ROPE = 32
MLA_V = 64
B_WIDTH = B_HEADS * MLA_V
PEER_HEADS = 8
PEER_NKEYS = 128
PEER_QDIM = 256
PEER_TOPK = 16
PEER_SEL = PEER_HEADS * PEER_TOPK
SCATTER_UNROLL = 32
MLA_TQ = 512
MLA_TK = 512

V7X_LANES = 128
V7X_SUBLANES = 8
V7X_VMEM_BYTES = 64 * 1024 * 1024
V7X_VMEM_REQUEST_CAP = 56 * 1024 * 1024


def _vmem_limit(pipelined_bytes, resident_bytes=0):
    est = 2 * int(pipelined_bytes) + int(resident_bytes)
    return int(min(max(est, 16 * 1024 * 1024), V7X_VMEM_REQUEST_CAP))


def _nbytes(shape, dtype):
    return int(np.prod(shape)) * jnp.dtype(dtype).itemsize


def _dot(a, b):
    return jnp.dot(a, b, preferred_element_type=F32)


def _dot_nt(a, b):
    return lax.dot_general(a, b, (((1,), (1,)), ((), ())), preferred_element_type=F32)


def _rms(x, g):
    ms = jnp.mean(x * x, axis=-1, keepdims=True)
    return x * lax.rsqrt(ms + NORM_EPS) * g


def _rope_tables(pos_f32, inv_row, lo_start, half, period):
    ang = pos_f32 * inv_row
    cos, sin = jnp.cos(ang), jnp.sin(ang)
    lane = lax.broadcasted_iota(I32, (1, V7X_LANES), 1) & (period - 1)
    is_lo = (lane >= lo_start) & (lane < lo_start + half)
    is_hi = (lane >= lo_start + half) & (lane < lo_start + 2 * half)
    c = jnp.where(is_lo | is_hi, cos, 1.0)
    s_lo = jnp.where(is_lo, -sin, 0.0)
    s_hi = jnp.where(is_hi, sin, 0.0)
    return c, s_lo, s_hi


def _apply_rope(x, tables, half):
    c, s_lo, s_hi = tables
    return (x * c + pltpu.roll(x, V7X_LANES - half, 1) * s_lo
            + pltpu.roll(x, half, 1) * s_hi)


def _mix_in_kernel(x_ref, pos_ref, gmix_ref, wa_ref, wcq_ref, wckv_ref, wkr_ref,
                   gcq_ref, wuq_ref, gckv_ref, wuk_ref, wuvt_ref, inva_ref, invm_ref,
                   qa_ref, ka_ref, va_ref, qm_ref, km_ref, vmt_ref, *, a_scale, m_scale):
    hn = _rms(x_ref[...], gmix_ref[...]).astype(BF16)
    pos = pos_ref[...].astype(F32)
    tab_a = _rope_tables(pos, inva_ref[...], 0, A_ROT_DIM // 2, HEAD_DIM)
    tab_m = _rope_tables(pos, invm_ref[...], MLA_NOPE, MLA_ROPE // 2, V7X_LANES)

    proj = _dot(hn, wa_ref[...])
    for c in range(A_WIDTH // V7X_LANES):
        sl = slice(c * V7X_LANES, (c + 1) * V7X_LANES)
        q = proj[:, sl]
        k = proj[:, A_WIDTH + c * V7X_LANES:A_WIDTH + (c + 1) * V7X_LANES]
        qa_ref[c] = _apply_rope(q, tab_a, A_ROT_DIM // 2) * a_scale
        ka_ref[c] = _apply_rope(k, tab_a, A_ROT_DIM // 2)
        va_ref[c] = proj[:, 2 * A_WIDTH + c * V7X_LANES:2 * A_WIDTH + (c + 1) * V7X_LANES]

    cq = _rms(_dot(hn, wcq_ref[...]), gcq_ref[...]).astype(BF16)
    qm = _dot(cq, wuq_ref[...])
    ckv = _rms(_dot(hn, wckv_ref[...]), gckv_ref[...]).astype(BF16)
    kn = _dot(ckv, wuk_ref[...])
    kr = _apply_rope(_dot(hn, wkr_ref[...]), tab_m, MLA_ROPE // 2)
    for h in range(B_HEADS):
        sl = slice(h * V7X_LANES, (h + 1) * V7X_LANES)
        qm_ref[:, sl] = (_apply_rope(qm[:, sl], tab_m, MLA_ROPE // 2) * m_scale).astype(BF16)
        km_ref[:, sl] = (kn[:, sl] + kr).astype(BF16)
    vmt_ref[...] = _dot_nt(wuvt_ref[...], ckv).astype(BF16).reshape(vmt_ref.shape)


def _mix_in(x2, pos2, g_mix, w_in, g_cq, w_uq, g_ckv, w_ukv, tm):
    T, D = x2.shape
    nope_rope = MLA_NOPE + MLA_ROPE
    wa = w_in[:, :3 * A_WIDTH].astype(BF16)
    o0 = 3 * A_WIDTH
    wcq = w_in[:, o0:o0 + MLA_Q_RANK].astype(BF16)
    wckv = w_in[:, o0 + MLA_Q_RANK:o0 + MLA_Q_RANK + MLA_KV_RANK].astype(BF16)
    wkr_raw = w_in[:, o0 + MLA_Q_RANK + MLA_KV_RANK:]
    wkr = jnp.pad(wkr_raw, ((0, 0), (MLA_NOPE, V7X_LANES - nope_rope))).astype(BF16)
    wuq = jnp.pad(w_uq.reshape(MLA_Q_RANK, B_HEADS, nope_rope),
                  ((0, 0), (0, 0), (0, V7X_LANES - nope_rope))
                  ).reshape(MLA_Q_RANK, B_HEADS * V7X_LANES).astype(BF16)
    wukv = w_ukv.reshape(MLA_KV_RANK, B_HEADS, MLA_NOPE + MLA_V)
    wuk = jnp.pad(wukv[:, :, :MLA_NOPE], ((0, 0), (0, 0), (0, V7X_LANES - MLA_NOPE))
                  ).reshape(MLA_KV_RANK, B_HEADS * V7X_LANES).astype(BF16)
    wuvt = wukv[:, :, MLA_NOPE:].reshape(MLA_KV_RANK, B_WIDTH).T.astype(BF16)
    lane = jnp.arange(V7X_LANES)
    ha, hm = A_ROT_DIM // 2, MLA_ROPE // 2
    inv_a = (ROPE_THETA ** (-jnp.arange(ha, dtype=F32) / ha))[lane % ha].reshape(1, V7X_LANES)
    inv_m = (ROPE_THETA ** (-jnp.arange(hm, dtype=F32) / hm))[lane % hm].reshape(1, V7X_LANES)

    full = lambda a: pl.BlockSpec(a.shape, lambda i: (0,) * a.ndim)
    row = lambda w: pl.BlockSpec((tm, w), lambda i: (i, 0))
    consts = (g_mix.reshape(1, D), wa, wcq, wckv, wkr, g_cq.reshape(1, -1), wuq,
              g_ckv.reshape(1, -1), wuk, wuvt, inv_a, inv_m)
    a_shape = (A_WIDTH // V7X_LANES, tm, V7X_LANES)
    m_width = B_HEADS * V7X_LANES
    vt_shape = (B_HEADS // 2, 2 * MLA_V, tm)
    blocks = (_nbytes((tm, D), F32) + sum(_nbytes(c.shape, c.dtype) for c in consts)
              + 3 * _nbytes(a_shape, F32) + 2 * _nbytes((tm, m_width), BF16)
              + _nbytes(vt_shape, BF16))
    temps = _nbytes((tm, 3 * A_WIDTH + 4 * B_HEADS * V7X_LANES), F32)
    a_spec = pl.BlockSpec(a_shape, lambda i: (0, i, 0))
    a_out = jax.ShapeDtypeStruct((a_shape[0], T, V7X_LANES), F32)
    return pl.pallas_call(
        functools.partial(_mix_in_kernel, a_scale=HEAD_DIM ** -0.5,
                          m_scale=nope_rope ** -0.5 * math.log2(math.e)),
        grid=(T // tm,),
        in_specs=[row(D), pl.BlockSpec((tm, 1), lambda i: (i, 0))] + [full(c) for c in consts],
        out_specs=[a_spec] * 3 + [row(m_width)] * 2
        + [pl.BlockSpec((None,) + vt_shape, lambda i: (i, 0, 0, 0))],
        out_shape=[a_out] * 3 + [jax.ShapeDtypeStruct((T, m_width), BF16)] * 2
        + [jax.ShapeDtypeStruct((T // tm,) + vt_shape, BF16)],
        compiler_params=pltpu.CompilerParams(
            dimension_semantics=("parallel",), vmem_limit_bytes=_vmem_limit(blocks, temps)),
        name="mix_in",
    )(x2, pos2, *consts)


def _dilated_kernel(q_ref, kp_ref, kc_ref, vp_ref, vc_ref, o_ref, *scratch, patterns, rows):
    n = pl.program_id(2)
    o_scr, l_scr = scratch[:len(patterns)], scratch[len(patterns):]
    lane = lax.broadcasted_iota(I32, (1, V7X_LANES), 1)
    lo = lane < HEAD_DIM
    qi = lax.broadcasted_iota(I32, (BLK, 2 * BLK), 0)
    kj = lax.broadcasted_iota(I32, (BLK, 2 * BLK), 1)
    dist = qi + BLK - kj

    for pi, (span, dil) in enumerate(patterns):
        band = (dist >= 0) & (dist <= span)
        first_valid = band & (kj >= jnp.where(n > 0, 0, BLK))
        step = BLK * dil

        def take(ref, start, dil=dil):
            idx = pl.ds(start, BLK, stride=dil) if dil > 1 else pl.ds(start, BLK)
            return ref[idx, :]

        for r in range(dil):
            for j in range(rows // step):
                start = j * step + r
                q2 = take(q_ref, start).astype(BF16)
                if j == 0:
                    kprev, vprev = take(kp_ref, rows - step + r), take(vp_ref, rows - step + r)
                    valid = first_valid
                else:
                    kprev, vprev = take(kc_ref, start - step), take(vc_ref, start - step)
                    valid = band
                k2 = jnp.concatenate([kprev, take(kc_ref, start)], axis=0).astype(BF16)
                v2 = jnp.concatenate([vprev, take(vc_ref, start)], axis=0).astype(BF16)
                acc = jnp.zeros((BLK, V7X_LANES), F32)
                lse2 = jnp.zeros((BLK, V7X_LANES), F32)
                for h in range(2):
                    sel = lo if h == 0 else ~lo
                    qh = jnp.where(sel, q2, jnp.zeros_like(q2))
                    vh = jnp.where(sel, v2, jnp.zeros_like(v2))
                    s = jnp.where(valid, _dot_nt(qh, k2), NEG_INF)
                    m = jnp.max(s, axis=-1, keepdims=True)
                    p = jnp.exp(s - m)
                    den = jnp.sum(p, axis=-1, keepdims=True)
                    acc = acc + _dot(p.astype(BF16), vh) / den
                    lse2 = jnp.where(sel, m + jnp.log(den), lse2)
                idx = pl.ds(start, BLK, stride=dil) if dil > 1 else pl.ds(start, BLK)
                o_scr[pi][idx, :] = acc
                l_scr[pi][idx, :] = lse2

    lses = [l[...] for l in l_scr]
    mx = functools.reduce(jnp.maximum, lses)
    es = [jnp.exp(l - mx) for l in lses]
    num = sum(e * o[...] for e, o in zip(es, o_scr))
    o_ref[...] = (num / sum(es)).astype(o_ref.dtype)


def _dilated(qa, ka, va, B, S):
    patterns = tuple((w // d, d) for w, d in DIL_PATTERNS)
    rows = BLK * max(d for _, d in patterns)
    assert all(span <= BLK for span, _ in patterns) and S % rows == 0
    nb = S // rows
    npairs = qa.shape[0]
    cur = pl.BlockSpec((None, rows, V7X_LANES), lambda b, h, n: (h, b * nb + n, 0))
    prev = pl.BlockSpec((None, rows, V7X_LANES),
                        lambda b, h, n: (h, b * nb + jnp.maximum(n - 1, 0), 0))
    blocks = 5 * _nbytes((rows, V7X_LANES), F32) + _nbytes((rows, V7X_LANES), BF16)
    scratch = 2 * len(patterns) * _nbytes((rows, V7X_LANES), F32)
    return pl.pallas_call(
        functools.partial(_dilated_kernel, patterns=patterns, rows=rows),
        grid=(B, npairs, nb),
        in_specs=[cur, prev, cur, prev, cur],
        out_specs=cur,
        out_shape=jax.ShapeDtypeStruct(qa.shape, BF16),
        scratch_shapes=[pltpu.VMEM((rows, V7X_LANES), F32)] * (2 * len(patterns)),
        compiler_params=pltpu.CompilerParams(
            dimension_semantics=("parallel", "parallel", "arbitrary"),
            vmem_limit_bytes=_vmem_limit(blocks, scratch + 4 * 1024 * 1024)),
        name="dilated_attention",
    )(qa, ka, ka, va, va)


def _mla_kernel(q_ref, k_ref, vt_ref, o_ref, s_ref, *, tq, tk):
    qi = pl.program_id(2)
    qs = [q_ref[:, h * V7X_LANES:(h + 1) * V7X_LANES] for h in range(2)]
    q_start = qi * tq

    def scores_into(slot, ki):
        kb = k_ref[pl.ds(pl.multiple_of(ki * tk, tk), tk), :]
        for h in range(2):
            s_ref[slot, h] = _dot_nt(kb[:, h * V7X_LANES:(h + 1) * V7X_LANES], qs[h])

    def consume(slot, ki, carry, masked, refill=None):
        ms, ls, accs = carry
        vtb = vt_ref[ki]
        new_ms, new_ls, alphas, ps = [], [], [], []
        for h in range(2):
            s = s_ref[slot, h]
            if masked:
                kpos = ki * tk + lax.broadcasted_iota(I32, (tk, tq), 0)
                qpos = q_start + lax.broadcasted_iota(I32, (tk, tq), 1)
                s = jnp.where(kpos <= qpos, s, NEG_INF)
            m_new = jnp.maximum(ms[h], jnp.max(s, axis=0, keepdims=True))
            alphas.append(jnp.exp2(ms[h] - m_new))
            p = jnp.exp2(s - m_new)
            new_ls.append(alphas[h] * ls[h] + jnp.sum(p, axis=0, keepdims=True))
            new_ms.append(m_new)
            ps.append(p.astype(BF16))
        if refill is not None:
            scores_into(slot, refill)
        new_accs = [accs[h] * alphas[h] + _dot(vtb[h * MLA_V:(h + 1) * MLA_V, :], ps[h])
                    for h in range(2)]
        return tuple(new_ms), tuple(new_ls), tuple(new_accs)

    def pair(j, carry):
        scores_into(1, 2 * j + 1)
        carry = consume(0, 2 * j, carry, False, refill=2 * j + 2)
        return consume(1, 2 * j + 1, carry, False)

    def even_tail(carry):
        return consume(0, n_full, carry, True)

    def odd_tail(carry):
        scores_into(1, n_full)
        carry = consume(0, n_full - 1, carry, False)
        return consume(1, n_full, carry, True)

    init = ((jnp.full((1, tq), NEG_INF, F32),) * 2, (jnp.zeros((1, tq), F32),) * 2,
            (jnp.zeros((MLA_V, tq), F32),) * 2)
    n_full = q_start // tk
    scores_into(0, 0)
    carry = lax.fori_loop(0, n_full // 2, pair, init)
    _, ls, accs = lax.cond(n_full % 2 == 0, even_tail, odd_tail, carry)
    for h in range(2):
        o_ref[h * MLA_V:(h + 1) * MLA_V, :] = (accs[h] / ls[h]).astype(o_ref.dtype)


def _mla(qm, km, vmt, B, S):
    tk = vmt.shape[-1]
    tq = min(MLA_TQ, S)
    assert tk % tq == 0 and S % tk == 0
    nk = S // tk
    hw = 2 * V7X_LANES
    q3, k3 = (a.reshape(B, S, -1) for a in (qm, km))
    blocks = _nbytes((tq, hw), BF16) + _nbytes((S, hw), BF16) + _nbytes((S, V7X_LANES), BF16) \
        + _nbytes((tq, V7X_LANES), BF16)
    score_scratch = (2, 2, tk, tq)
    temps = _nbytes(score_scratch, F32) + 6 * _nbytes((tk, tq), F32)
    return pl.pallas_call(
        functools.partial(_mla_kernel, tq=tq, tk=tk),
        grid=(B, B_HEADS // 2, S // tq),
        in_specs=[pl.BlockSpec((None, tq, hw), lambda b, h, i: (b, i, h)),
                  pl.BlockSpec((None, S, hw), lambda b, h, i: (b, 0, h)),
                  pl.BlockSpec((nk, None, V7X_LANES, tk), lambda b, h, i: (b, h, 0, 0))],
        out_specs=pl.BlockSpec((None, V7X_LANES, tq), lambda b, h, i: (b, h, i)),
        out_shape=jax.ShapeDtypeStruct((B, B_WIDTH, S), BF16),
        scratch_shapes=[pltpu.VMEM(score_scratch, F32)],
        compiler_params=pltpu.CompilerParams(
            dimension_semantics=("parallel", "parallel", "arbitrary"),
            vmem_limit_bytes=_vmem_limit(blocks, temps)),
        name="mla_attention",
    )(q3, k3, vmt)


def _mix_out_kernel(oa_ref, om_ref, x_ref, wo_ref, gffn_ref, wpq_ref, keys_ref,
                    h_ref, hn_ref, i1_ref, i2_ref, gate_ref):
    oa = jnp.concatenate([oa_ref[c] for c in range(oa_ref.shape[0])], axis=1)
    mixed_m = lax.dot_general(om_ref[...], wo_ref[A_WIDTH:, :], (((0,), (0,)), ((), ())),
                              preferred_element_type=F32)
    h = x_ref[...] + _dot(oa, wo_ref[:A_WIDTH, :]) + mixed_m
    h_ref[...] = h
    hn = _rms(h, gffn_ref[...]).astype(BF16)
    hn_ref[...] = hn
    _peer_topk(lambda head: _dot(hn, wpq_ref[:, head * PEER_QDIM:(head + 1) * PEER_QDIM]),
               keys_ref, i1_ref, i2_ref, gate_ref)


def _mix_out(oa4, omt, x2, w_out, g_ffn, w_pq, sub_keys):
    T, D = x2.shape
    S = omt.shape[-1]
    tm = min(256, S)
    nt = S // tm
    wo = w_out.astype(BF16)
    wpq = w_pq.astype(BF16)
    keys = sub_keys.reshape(PEER_HEADS * 2, PEER_NKEYS, PEER_QDIM // 2).astype(BF16)
    QW = wpq.shape[1]
    row = lambda w: pl.BlockSpec((tm, w), lambda i: (i, 0))
    col = pl.BlockSpec((PEER_SEL, tm), lambda i: (0, i))
    full = lambda a: pl.BlockSpec(a.shape, lambda i: (0,) * a.ndim)
    blocks = _nbytes((tm, A_WIDTH), BF16) + _nbytes((tm, B_WIDTH), BF16) \
        + 2 * _nbytes((tm, D), F32) + _nbytes(wo.shape, BF16) + _nbytes(wpq.shape, BF16) \
        + _nbytes(keys.shape, BF16) + _nbytes((tm, D), BF16) + 3 * _nbytes((PEER_SEL, tm), F32)
    temps = 5 * _nbytes((tm, QW), F32) + 16 * _nbytes((2 * PEER_SEL, tm), F32)
    return pl.pallas_call(
        _mix_out_kernel,
        grid=(T // tm,),
        in_specs=[pl.BlockSpec((oa4.shape[0], tm, V7X_LANES), lambda i: (0, i, 0)),
                  pl.BlockSpec((None, B_WIDTH, tm), lambda i: (i // nt, 0, i % nt)), row(D), full(wo),
                  pl.BlockSpec((1, D), lambda i: (0, 0)), full(wpq), full(keys)],
        out_specs=[row(D), row(D), col, col, col],
        out_shape=[jax.ShapeDtypeStruct((T, D), F32), jax.ShapeDtypeStruct((T, D), BF16),
                   jax.ShapeDtypeStruct((PEER_SEL, T), I32), jax.ShapeDtypeStruct((PEER_SEL, T), I32),
                   jax.ShapeDtypeStruct((PEER_SEL, T), F32)],
        compiler_params=pltpu.CompilerParams(
            dimension_semantics=("parallel",), vmem_limit_bytes=_vmem_limit(blocks, temps)),
        name="mix_out_topk",
    )(oa4, omt, x2, wo, g_ffn.reshape(1, D), wpq, keys)


def _topk_rows(sc, k, order=None, payload=None):
    if order is None:
        order = lax.broadcasted_iota(I32, sc.shape, 0)
    out_rows = lax.broadcasted_iota(I32, (k, sc.shape[1]), 0)
    vals = jnp.zeros((k, sc.shape[1]), F32)
    sel = jnp.zeros((k, sc.shape[1]), I32)
    for i in range(k):
        m = jnp.max(sc, axis=0, keepdims=True)
        first = jnp.min(jnp.where(sc == m, order, _ORDER_NONE), axis=0, keepdims=True)
        hit = order == first
        if payload is None:
            got = first
        else:
            got = jnp.max(jnp.where(hit, payload, -1), axis=0, keepdims=True)
        vals = jnp.where(out_rows == i, m, vals)
        sel = jnp.where(out_rows == i, got, sel)
        sc = jnp.where(hit, -jnp.inf, sc)
    return vals, sel


_ORDER_NONE = 1 << 20


def _candidate_pieces(k):
    pieces = []
    split = 0
    while (split + 1) * (split + 1) <= k:
        split += 1
    for r in range(split):
        n = k // (r + 1)
        for s0 in range(0, n, V7X_SUBLANES):
            pieces.append((True, r, s0, 0, min(V7X_SUBLANES, n - s0)))
    for s in range(split):
        n = k // (s + 1)
        for r0 in range(0, n, V7X_SUBLANES):
            lo, hi = max(split - r0, 0), min(V7X_SUBLANES, n - r0)
            if hi > lo:
                pieces.append((False, s, r0, lo, hi))
    return pieces


def _peer_topk(project, keys_ref, i1_ref, i2_ref, gate_ref):
    K = PEER_TOPK
    tt = i1_ref.shape[1]
    pieces = _candidate_pieces(K)
    j8 = lax.broadcasted_iota(I32, (V7X_SUBLANES, tt), 0)
    q_next = project(0)
    for h in range(PEER_HEADS):
        q_head = q_next
        if h + 1 < PEER_HEADS:
            q_next = project(h + 1)
        vals, idxs = [], []
        for p in range(2):
            q = q_head[:, p * (PEER_QDIM // 2):(p + 1) * (PEER_QDIM // 2)].astype(BF16)
            sc = _dot_nt(keys_ref[2 * h + p], q)
            v, i = _topk_rows(sc, K)
            vals.append(v)
            idxs.append(i)
        cand, order, pair = [], [], []
        for n, (along_s, fixed, start, lo, hi) in enumerate(pieces):
            a, b = (0, 1) if along_s else (1, 0)
            sl = slice(start, start + V7X_SUBLANES)
            valid = (j8 >= lo) & (j8 < hi)
            cand.append(jnp.where(valid, vals[a][fixed:fixed + 1, :] + vals[b][sl, :], -jnp.inf))
            flat = (fixed * K + start + j8) if along_s else ((start + j8) * K + fixed)
            order.append(jnp.where(valid, flat, _ORDER_NONE - 1 - n * V7X_SUBLANES - j8))
            e1 = idxs[0][fixed:fixed + 1, :] if along_s else idxs[0][sl, :]
            e2 = idxs[1][sl, :] if along_s else idxs[1][fixed:fixed + 1, :]
            pair.append(e1 * PEER_NKEYS + e2)
        best, expert = _topk_rows(jnp.concatenate(cand, axis=0), K,
                                  order=jnp.concatenate(order, axis=0),
                                  payload=jnp.concatenate(pair, axis=0))
        e = jnp.exp(best - jnp.max(best, axis=0, keepdims=True))
        rows = slice(h * K, (h + 1) * K)
        gate_ref[rows, :] = e / jnp.sum(e, axis=0, keepdims=True)
        i1_ref[rows, :] = expert // PEER_NKEYS
        i2_ref[rows, :] = expert % PEER_NKEYS


def _peer_act_kernel(hn_ref, u_ref, i1_ref, i2_ref, gate_ref, coef_ref, z_ref, *, chunks):
    g = pl.program_id(1)

    @pl.when(g == 0)
    def _():
        z_ref[...] = jnp.zeros_like(z_ref)

    hn = hn_ref[...]
    i1 = i1_ref[...]
    i2 = i2_ref[...]
    zsel = z_ref[...]
    for k in range(chunks // 2):
        z = _dot_nt(hn, u_ref[2 * k * PEER_NKEYS:(2 * k + 2) * PEER_NKEYS, :])
        for j in range(2):
            picked = jnp.take_along_axis(z[:, j * PEER_NKEYS:(j + 1) * PEER_NKEYS], i2, axis=1)
            zsel = jnp.where(i1 == g * chunks + 2 * k + j, picked, zsel)
    z_ref[...] = zsel

    @pl.when(g == pl.num_programs(1) - 1)
    def _():
        act = 0.5 * zsel * (1.0 + lax.erf(zsel * np.float32(math.sqrt(0.5))))
        coef_ref[...] = act * gate_ref[...]


def _peer_act(hn, u_bf, i1, i2, gate):
    T, D = hn.shape
    E = u_bf.shape[0]
    tm = min(1024, T)
    chunks = 32
    rows = chunks * PEER_NKEYS
    sel = pl.BlockSpec((tm, PEER_SEL), lambda i, g: (i, 0))
    blocks = _nbytes((tm, D), BF16) + _nbytes((rows, D), BF16) + 4 * _nbytes((tm, PEER_SEL), F32)
    return pl.pallas_call(
        functools.partial(_peer_act_kernel, chunks=chunks),
        grid=(T // tm, E // rows),
        in_specs=[pl.BlockSpec((tm, D), lambda i, g: (i, 0)),
                  pl.BlockSpec((rows, D), lambda i, g: (g, 0)), sel, sel, sel],
        out_specs=sel,
        out_shape=jax.ShapeDtypeStruct((T, PEER_SEL), F32),
        scratch_shapes=[pltpu.VMEM((tm, PEER_SEL), F32)],
        compiler_params=pltpu.CompilerParams(
            dimension_semantics=("parallel", "arbitrary"),
            vmem_limit_bytes=_vmem_limit(
                blocks, (chunks // 2 + 4) * _nbytes((tm, 2 * PEER_NKEYS), F32))),
        name="peer_act",
    )(hn, u_bf, i1, i2, gate)


def _peer_out_kernel(i1_ref, i2_ref, coef_ref, v_ref, h_ref, gfin_ref, out_ref,
                     sx_ref, acc_ref, *, tt, pitch, chunks):
    NK = PEER_NKEYS
    sub = lax.broadcasted_iota(I32, (NK, PEER_SEL), 0)

    def scatter_token(t, carry):
        i1r = i1_ref[pl.ds(t, 1), :]
        i2r = i2_ref[pl.ds(t, 1), :]
        cr = coef_ref[pl.ds(t, 1), :]
        a_t = jnp.where(sub == i1r, 1.0, 0.0).astype(BF16)
        b_t = jnp.where(sub == i2r, cr, 0.0).astype(BF16)
        s = _dot_nt(a_t, b_t)
        sx_ref[pl.ds(pl.multiple_of(t * pitch, V7X_SUBLANES), NK), :] = s
        return carry

    lax.fori_loop(0, tt, scatter_token, 0, unroll=SCATTER_UNROLL)

    acc_ref[...] = jnp.zeros_like(acc_ref)

    def chunk_group(g, carry):
        part = jnp.zeros(acc_ref.shape, F32)
        for cc in range(chunks // 2):
            c0 = g * chunks + 2 * cc
            lhs = jnp.concatenate(
                [sx_ref[pl.ds(c0 + k, tt, stride=pitch), :] for k in range(2)],
                axis=1).astype(BF16)
            rhs = v_ref[pl.ds(pl.multiple_of(c0 * NK, 2 * NK), 2 * NK), :]
            part = part + _dot(lhs, rhs)
        acc_ref[...] += part
        return carry

    lax.fori_loop(0, v_ref.shape[0] // (chunks * NK), chunk_group, 0)
    out_ref[...] = _rms(h_ref[...] + acc_ref[...], gfin_ref[...])


def _peer_out(i1, i2, coef, v_bf, h, g_final):
    T, D = h.shape
    E = v_bf.shape[0]
    tt = min(256, T)
    chunks = 16
    pitch = PEER_NKEYS + V7X_SUBLANES
    sel = pl.BlockSpec((tt, PEER_SEL), lambda i: (i, 0))
    tok = pl.BlockSpec((tt, D), lambda i: (i, 0))
    blocks = 3 * _nbytes((tt, PEER_SEL), F32) + 2 * _nbytes((tt, D), F32)
    scratch = _nbytes((tt * pitch, PEER_NKEYS), F32) + _nbytes((tt, D), F32)
    resident = _nbytes((E, D), BF16)
    return pl.pallas_call(
        functools.partial(_peer_out_kernel, tt=tt, pitch=pitch, chunks=chunks),
        grid=(T // tt,),
        in_specs=[sel, sel, sel,
                  pl.BlockSpec((E, D), lambda i: (0, 0), pipeline_mode=pl.Buffered(1)), tok,
                  pl.BlockSpec((1, D), lambda i: (0, 0))],
        out_specs=tok,
        out_shape=jax.ShapeDtypeStruct((T, D), F32),
        scratch_shapes=[pltpu.VMEM((tt * pitch, PEER_NKEYS), F32),
                        pltpu.VMEM((tt, D), F32)],
        compiler_params=pltpu.CompilerParams(
            dimension_semantics=("parallel",),
            vmem_limit_bytes=_vmem_limit(blocks, resident + scratch + 2 * _nbytes((tt, D), F32))),
        name="peer_out",
    )(i1, i2, coef, v_bf, h, g_final.reshape(1, D))


def kernel(x, positions, g_mix, w_in, g_cq, w_uq, g_ckv, w_ukv, w_out, g_ffn,
           w_pq, sub_keys, u_emb, v_emb, g_final):
    B, S, D = x.shape
    T = B * S
    assert g_mix.shape[0] == 1, "single-layer configuration"
    x2 = x.reshape(T, D)
    pos2 = positions.reshape(T, 1)

    qa, ka, va, qm, km, vmt = _mix_in(x2, pos2, g_mix[0], w_in[0], g_cq[0], w_uq[0],
                                      g_ckv[0], w_ukv[0], tm=min(MLA_TK, S))
    oa4 = _dilated(qa, ka, va, B, S)
    omt = _mla(qm, km, vmt, B, S)
    h, hn, i1, i2, gate = _mix_out(oa4, omt, x2, w_out[0], g_ffn[0], w_pq[0], sub_keys[0])
    i1, i2, gate = i1.T, i2.T, gate.T
    coef = _peer_act(hn, u_emb[0].astype(BF16), i1, i2, gate)
    out = _peer_out(i1, i2, coef, v_emb[0].astype(BF16), h, g_final)
    return out.reshape(B, S, D)
```

```python
import functools
import math

import jax
import jax.numpy as jnp
import numpy as np
from jax import lax
from jax.experimental import pallas as pl
from jax.experimental.pallas import tpu as pltpu

F32 = jnp.float32
BF16 = jnp.bfloat16
I32 = jnp.int32

ROPE_THETA = 500000.0
NORM_EPS = 1e-6
NEG_INF = -1e30
BLK = 128
A_HEADS = 8
HEAD_DIM = 64
A_ROT_DIM = HEAD_DIM // 4
DIL_PATTERNS = ((128, 1), (512, 4), (2048, 16))
A_WIDTH = A_HEADS * HEAD_DIM
B_HEADS = 8
MLA_Q_RANK = 256
MLA_KV_RANK = 128
MLA_NOPE = 64
MLA_ROPE = 32
MLA_V = 64
B_WIDTH = B_HEADS * MLA_V
PEER_HEADS = 8
PEER_NKEYS = 128
PEER_QDIM = 256
PEER_TOPK = 16
PEER_SEL = PEER_HEADS * PEER_TOPK
SCATTER_UNROLL = 64
MLA_TQ = 512
MLA_TK = 512

V7X_LANES = 128
V7X_SUBLANES = 8
V7X_VMEM_BYTES = 64 * 1024 * 1024
V7X_VMEM_REQUEST_CAP = 56 * 1024 * 1024


def _vmem_limit(pipelined_bytes, resident_bytes=0):
    est = 2 * int(pipelined_bytes) + int(resident_bytes)
    return int(min(max(est, 16 * 1024 * 1024), V7X_VMEM_REQUEST_CAP))


def _nbytes(shape, dtype):
    return int(np.prod(shape)) * jnp.dtype(dtype).itemsize


def _dot(a, b):
    return jnp.dot(a, b, preferred_element_type=F32)


def _dot_nt(a, b):
    return lax.dot_general(a, b, (((1,), (1,)), ((), ())), preferred_element_type=F32)


def _rms(x, g):
    ms = jnp.mean(x * x, axis=-1, keepdims=True)
    return x * lax.rsqrt(ms + NORM_EPS) * g


def _rope_tables(cos, sin, lo_start, half, period):
    lane = lax.broadcasted_iota(I32, (1, V7X_LANES), 1) & (period - 1)
    is_lo = (lane >= lo_start) & (lane < lo_start + half)
    is_hi = (lane >= lo_start + half) & (lane < lo_start + 2 * half)
    c = jnp.where(is_lo | is_hi, cos, 1.0)
    s_lo = jnp.where(is_lo, -sin, 0.0)
    s_hi = jnp.where(is_hi, sin, 0.0)
    return c, s_lo, s_hi


def _apply_rope(x, tables, half):
    c, s_lo, s_hi = tables
    return (x * c + pltpu.roll(x, V7X_LANES - half, 1) * s_lo
            + pltpu.roll(x, half, 1) * s_hi)


def _mix_in_kernel(x_ref, pos_ref, gmix_ref, wa_ref, wcq_ref, wckv_ref, wkr_ref,
                   gcq_ref, wuq_ref, gckv_ref, wuk_ref, wuvt_ref, inv_ref,
                   qa_ref, ka_ref, va_ref, qm_ref, km_ref, vmt_ref, *, a_scale, m_scale):
    hn = _rms(x_ref[...], gmix_ref[...]).astype(BF16)
    ang = pos_ref[...].astype(F32) * inv_ref[...]
    cos, sin = jnp.cos(ang), jnp.sin(ang)
    first_head = lax.broadcasted_iota(I32, (1, V7X_LANES), 1) < HEAD_DIM
    cos_a = jnp.where(first_head, cos, pltpu.roll(cos, HEAD_DIM, 1))
    sin_a = jnp.where(first_head, sin, pltpu.roll(sin, HEAD_DIM, 1))
    tab_a = _rope_tables(cos_a, sin_a, 0, A_ROT_DIM // 2, HEAD_DIM)
    tab_m = _rope_tables(cos, sin, MLA_NOPE, MLA_ROPE // 2, V7X_LANES)

    proj = _dot(hn, wa_ref[...])
    for c in range(A_WIDTH // V7X_LANES):
        sl = slice(c * V7X_LANES, (c + 1) * V7X_LANES)
        q = proj[:, sl]
        k = proj[:, A_WIDTH + c * V7X_LANES:A_WIDTH + (c + 1) * V7X_LANES]
        qa_ref[c] = _apply_rope(q, tab_a, A_ROT_DIM // 2) * a_scale
        ka_ref[c] = _apply_rope(k, tab_a, A_ROT_DIM // 2)
        va_ref[c] = proj[:, 2 * A_WIDTH + c * V7X_LANES:2 * A_WIDTH + (c + 1) * V7X_LANES]

    cq = _rms(_dot(hn, wcq_ref[...]), gcq_ref[...]).astype(BF16)
    qm = _dot(cq, wuq_ref[...])
    ckv = _rms(_dot(hn, wckv_ref[...]), gckv_ref[...]).astype(BF16)
    kn = _dot(ckv, wuk_ref[...])
    kr = _apply_rope(_dot(hn, wkr_ref[...]), tab_m, MLA_ROPE // 2)
    for h in range(B_HEADS):
        sl = slice(h * V7X_LANES, (h + 1) * V7X_LANES)
        qm_ref[:, sl] = (_apply_rope(qm[:, sl], tab_m, MLA_ROPE // 2) * m_scale).astype(BF16)
        km_ref[:, sl] = (kn[:, sl] + kr).astype(BF16)
    vmt_ref[...] = _dot_nt(wuvt_ref[...], ckv).astype(BF16).reshape(vmt_ref.shape)


def _mix_in(x2, pos2, g_mix, w_in, g_cq, w_uq, g_ckv, w_ukv, tm):
    T, D = x2.shape
    nope_rope = MLA_NOPE + MLA_ROPE
    wa = w_in[:, :3 * A_WIDTH].astype(BF16)
    o0 = 3 * A_WIDTH
    wcq = w_in[:, o0:o0 + MLA_Q_RANK].astype(BF16)
    wckv = w_in[:, o0 + MLA_Q_RANK:o0 + MLA_Q_RANK + MLA_KV_RANK].astype(BF16)
    wkr_raw = w_in[:, o0 + MLA_Q_RANK + MLA_KV_RANK:]
    wkr = jnp.pad(wkr_raw, ((0, 0), (MLA_NOPE, V7X_LANES - nope_rope))).astype(BF16)
    wuq = jnp.pad(w_uq.reshape(MLA_Q_RANK, B_HEADS, nope_rope),
                  ((0, 0), (0, 0), (0, V7X_LANES - nope_rope))
                  ).reshape(MLA_Q_RANK, B_HEADS * V7X_LANES).astype(BF16)
    wukv = w_ukv.reshape(MLA_KV_RANK, B_HEADS, MLA_NOPE + MLA_V)
    wuk = jnp.pad(wukv[:, :, :MLA_NOPE], ((0, 0), (0, 0), (0, V7X_LANES - MLA_NOPE))
                  ).reshape(MLA_KV_RANK, B_HEADS * V7X_LANES).astype(BF16)
    wuvt = wukv[:, :, MLA_NOPE:].reshape(MLA_KV_RANK, B_WIDTH).T.astype(BF16)
    lane = jnp.arange(V7X_LANES)
    ha, hm = A_ROT_DIM // 2, MLA_ROPE // 2
    inv_a = (ROPE_THETA ** (-jnp.arange(ha, dtype=F32) / ha))[lane % ha]
    inv_m = (ROPE_THETA ** (-jnp.arange(hm, dtype=F32) / hm))[lane % hm]
    inv = jnp.where(lane < A_ROT_DIM, inv_a,
                    jnp.where((lane >= MLA_NOPE) & (lane < nope_rope), inv_m, 0.0)
                    ).reshape(1, V7X_LANES)

    full = lambda a: pl.BlockSpec(a.shape, lambda i: (0,) * a.ndim)
    row = lambda w: pl.BlockSpec((tm, w), lambda i: (i, 0))
    consts = (g_mix.reshape(1, D), wa, wcq, wckv, wkr, g_cq.reshape(1, -1), wuq,
              g_ckv.reshape(1, -1), wuk, wuvt, inv)
    a_shape = (A_WIDTH // V7X_LANES, tm, V7X_LANES)
    m_width = B_HEADS * V7X_LANES
    vt_shape = (B_HEADS // 2, 2 * MLA_V, tm)
    blocks = (_nbytes((tm, D), F32) + sum(_nbytes(c.shape, c.dtype) for c in consts)
              + 3 * _nbytes(a_shape, F32) + 2 * _nbytes((tm, m_width), BF16)
              + _nbytes(vt_shape, BF16))
    temps = _nbytes((tm, 3 * A_WIDTH + 4 * B_HEADS * V7X_LANES), F32)
    a_spec = pl.BlockSpec(a_shape, lambda i: (0, i, 0))
    a_out = jax.ShapeDtypeStruct((a_shape[0], T, V7X_LANES), F32)
    return pl.pallas_call(
        functools.partial(_mix_in_kernel, a_scale=HEAD_DIM ** -0.5,
                          m_scale=nope_rope ** -0.5 * math.log2(math.e)),
        grid=(T // tm,),
        in_specs=[row(D), pl.BlockSpec((tm, 1), lambda i: (i, 0))] + [full(c) for c in consts],
        out_specs=[a_spec] * 3 + [row(m_width)] * 2
        + [pl.BlockSpec((None,) + vt_shape, lambda i: (i, 0, 0, 0))],
        out_shape=[a_out] * 3 + [jax.ShapeDtypeStruct((T, m_width), BF16)] * 2
        + [jax.ShapeDtypeStruct((T // tm,) + vt_shape, BF16)],
        compiler_params=pltpu.CompilerParams(
            dimension_semantics=("parallel",), vmem_limit_bytes=_vmem_limit(blocks, temps)),
        name="mix_in",
    )(x2, pos2, *consts)


def _dilated_kernel(q_ref, kp_ref, kc_ref, vp_ref, vc_ref, o_ref, *scratch, patterns, rows):
    n = pl.program_id(2)
    o_scr, l_scr = scratch[:len(patterns)], scratch[len(patterns):]
    lane = lax.broadcasted_iota(I32, (1, V7X_LANES), 1)
    lo = lane < HEAD_DIM
    qi = lax.broadcasted_iota(I32, (BLK, 2 * BLK), 0)
    kj = lax.broadcasted_iota(I32, (BLK, 2 * BLK), 1)
    dist = qi + BLK - kj

    for pi, (span, dil) in enumerate(patterns):
        band = (dist >= 0) & (dist <= span)
        first_valid = band & (kj >= jnp.where(n > 0, 0, BLK))
        step = BLK * dil

        def take(ref, start, dil=dil):
            idx = pl.ds(start, BLK, stride=dil) if dil > 1 else pl.ds(start, BLK)
            return ref[idx, :]

        for r in range(dil):
            for j in range(rows // step):
                start = j * step + r
                q2 = take(q_ref, start).astype(BF16)
                if j == 0:
                    kprev, vprev = take(kp_ref, rows - step + r), take(vp_ref, rows - step + r)
                    valid = first_valid
                else:
                    kprev, vprev = take(kc_ref, start - step), take(vc_ref, start - step)
                    valid = band
                k2 = jnp.concatenate([kprev, take(kc_ref, start)], axis=0).astype(BF16)
                v2 = jnp.concatenate([vprev, take(vc_ref, start)], axis=0).astype(BF16)
                acc = jnp.zeros((BLK, V7X_LANES), F32)
                lse2 = jnp.zeros((BLK, V7X_LANES), F32)
                for h in range(2):
                    sel = lo if h == 0 else ~lo
                    qh = jnp.where(sel, q2, jnp.zeros_like(q2))
                    vh = jnp.where(sel, v2, jnp.zeros_like(v2))
                    s = jnp.where(valid, _dot_nt(qh, k2), NEG_INF)
                    m = jnp.max(s, axis=-1, keepdims=True)
                    p = jnp.exp(s - m)
                    den = jnp.sum(p, axis=-1, keepdims=True)
                    acc = acc + _dot(p.astype(BF16), vh) / den
                    lse2 = jnp.where(sel, m + jnp.log(den), lse2)
                idx = pl.ds(start, BLK, stride=dil) if dil > 1 else pl.ds(start, BLK)
                o_scr[pi][idx, :] = acc
                l_scr[pi][idx, :] = lse2

    lses = [l[...] for l in l_scr]
    mx = functools.reduce(jnp.maximum, lses)
    es = [jnp.exp(l - mx) for l in lses]
    num = sum(e * o[...] for e, o in zip(es, o_scr))
    o_ref[...] = (num / sum(es)).astype(o_ref.dtype)


def _dilated(qa, ka, va, B, S):
    patterns = tuple((w // d, d) for w, d in DIL_PATTERNS)
    rows = BLK * max(d for _, d in patterns)
    assert all(span <= BLK for span, _ in patterns) and S % rows == 0
    nb = S // rows
    npairs = qa.shape[0]
    cur = pl.BlockSpec((None, rows, V7X_LANES), lambda b, h, n: (h, b * nb + n, 0))
    prev = pl.BlockSpec((None, rows, V7X_LANES),
                        lambda b, h, n: (h, b * nb + jnp.maximum(n - 1, 0), 0))
    blocks = 5 * _nbytes((rows, V7X_LANES), F32) + _nbytes((rows, V7X_LANES), BF16)
    scratch = 2 * len(patterns) * _nbytes((rows, V7X_LANES), F32)
    return pl.pallas_call(
        functools.partial(_dilated_kernel, patterns=patterns, rows=rows),
        grid=(B, npairs, nb),
        in_specs=[cur, prev, cur, prev, cur],
        out_specs=cur,
        out_shape=jax.ShapeDtypeStruct(qa.shape, BF16),
        scratch_shapes=[pltpu.VMEM((rows, V7X_LANES), F32)] * (2 * len(patterns)),
        compiler_params=pltpu.CompilerParams(
            dimension_semantics=("parallel", "parallel", "arbitrary"),
            vmem_limit_bytes=_vmem_limit(blocks, scratch + 4 * 1024 * 1024)),
        name="dilated_attention",
    )(qa, ka, ka, va, va)


def _mla_kernel(q_ref, k_ref, vt_ref, o_ref, s_ref, *, tq, tk):
    qi = pl.program_id(2)
    qs = [q_ref[:, h * V7X_LANES:(h + 1) * V7X_LANES] for h in range(2)]
    q_start = qi * tq

    def scores_into(slot, ki):
        kb = k_ref[pl.ds(pl.multiple_of(ki * tk, tk), tk), :]
        for h in range(2):
            s_ref[slot, h] = _dot_nt(kb[:, h * V7X_LANES:(h + 1) * V7X_LANES], qs[h])

    def consume(slot, ki, carry, masked, refill=None):
        ms, ls, accs = carry
        vtb = vt_ref[ki]
        new_ms, new_ls, alphas, ps = [], [], [], []
        for h in range(2):
            s = s_ref[slot, h]
            if masked:
                kpos = ki * tk + lax.broadcasted_iota(I32, (tk, tq), 0)
                qpos = q_start + lax.broadcasted_iota(I32, (tk, tq), 1)
                s = jnp.where(kpos <= qpos, s, NEG_INF)
            m_new = jnp.maximum(ms[h], jnp.max(s, axis=0, keepdims=True))
            alphas.append(jnp.exp2(ms[h] - m_new))
            p = jnp.exp2(s - m_new)
            new_ls.append(alphas[h] * ls[h] + jnp.sum(p, axis=0, keepdims=True))
            new_ms.append(m_new)
            ps.append(p.astype(BF16))
        if refill is not None:
            scores_into(slot, refill)
        new_accs = [accs[h] * alphas[h] + _dot(vtb[h * MLA_V:(h + 1) * MLA_V, :], ps[h])
                    for h in range(2)]
        return tuple(new_ms), tuple(new_ls), tuple(new_accs)

    def pair(j, carry):
        scores_into(1, 2 * j + 1)
        carry = consume(0, 2 * j, carry, False, refill=2 * j + 2)
        return consume(1, 2 * j + 1, carry, False)

    def even_tail(carry):
        return consume(0, n_full, carry, True)

    def odd_tail(carry):
        scores_into(1, n_full)
        carry = consume(0, n_full - 1, carry, False)
        return consume(1, n_full, carry, True)

    init = ((jnp.full((1, tq), NEG_INF, F32),) * 2, (jnp.zeros((1, tq), F32),) * 2,
            (jnp.zeros((MLA_V, tq), F32),) * 2)
    n_full = q_start // tk
    scores_into(0, 0)
    carry = lax.fori_loop(0, n_full // 2, pair, init)
    if tq == tk:
        _, ls, accs = lax.cond(n_full % 2 == 0, even_tail, odd_tail, carry)
    else:
        scores_into(1, n_full + 1)
        carry = consume(0, n_full, carry, True)
        _, ls, accs = consume(1, n_full + 1, carry, True)
    for h in range(2):
        o_ref[h * MLA_V:(h + 1) * MLA_V, :] = (accs[h] / ls[h]).astype(o_ref.dtype)


def _mla(qm, km, vmt, B, S):
    tk = vmt.shape[-1]
    tq = min(MLA_TQ, S)
    assert tq in (tk, 2 * tk) and S % tq == 0
    nk = S // tk
    hw = 2 * V7X_LANES
    q3, k3 = (a.reshape(B, S, -1) for a in (qm, km))
    blocks = _nbytes((tq, hw), BF16) + _nbytes((S, hw), BF16) + _nbytes((S, V7X_LANES), BF16) \
        + _nbytes((tq, V7X_LANES), BF16)
    score_scratch = (2, 2, tk, tq)
    temps = _nbytes(score_scratch, F32) + 6 * _nbytes((tk, tq), F32)
    return pl.pallas_call(
        functools.partial(_mla_kernel, tq=tq, tk=tk),
        grid=(B, B_HEADS // 2, S // tq),
        in_specs=[pl.BlockSpec((None, tq, hw), lambda b, h, i: (b, i, h)),
                  pl.BlockSpec((None, S, hw), lambda b, h, i: (b, 0, h)),
                  pl.BlockSpec((nk, None, V7X_LANES, tk), lambda b, h, i: (b, h, 0, 0))],
        out_specs=pl.BlockSpec((None, V7X_LANES, tq), lambda b, h, i: (b, h, i)),
        out_shape=jax.ShapeDtypeStruct((B, B_WIDTH, S), BF16),
        scratch_shapes=[pltpu.VMEM(score_scratch, F32)],
        compiler_params=pltpu.CompilerParams(
            dimension_semantics=("parallel", "parallel", "arbitrary"),
            vmem_limit_bytes=_vmem_limit(blocks, temps)),
        name="mla_attention",
    )(q3, k3, vmt)


def _mix_out_kernel(oa_ref, om_ref, x_ref, wo_ref, gffn_ref, wpq_ref, keys_ref,
                    h_ref, hn_ref, i1_ref, i2_ref, gate_ref):
    oa = jnp.concatenate([oa_ref[c] for c in range(oa_ref.shape[0])], axis=1)
    mixed_m = lax.dot_general(om_ref[...], wo_ref[A_WIDTH:, :], (((0,), (0,)), ((), ())),
                              preferred_element_type=F32)
    h = x_ref[...] + _dot(oa, wo_ref[:A_WIDTH, :]) + mixed_m
    h_ref[...] = h
    hn = _rms(h, gffn_ref[...]).astype(BF16)
    hn_ref[...] = hn
    i1, i2, gate = _peer_topk(
        lambda head: _dot(hn, wpq_ref[:, head * PEER_QDIM:(head + 1) * PEER_QDIM]), keys_ref)
    i1_ref[...], i2_ref[...], gate_ref[...] = i1.T, i2.T, gate.T


def _mix_out(oa4, omt, x2, w_out, g_ffn, w_pq, sub_keys):
    T, D = x2.shape
    S = omt.shape[-1]
    tm = min(512, S)
    nt = S // tm
    wo = w_out.astype(BF16)
    wpq = w_pq.astype(BF16)
    keys = sub_keys.reshape(PEER_HEADS * 2, PEER_NKEYS, PEER_QDIM // 2).astype(BF16)
    QW = wpq.shape[1]
    row = lambda w: pl.BlockSpec((tm, w), lambda i: (i, 0))
    full = lambda a: pl.BlockSpec(a.shape, lambda i: (0,) * a.ndim)
    blocks = _nbytes((tm, A_WIDTH), BF16) + _nbytes((tm, B_WIDTH), BF16) \
        + 2 * _nbytes((tm, D), F32) + _nbytes(wo.shape, BF16) + _nbytes(wpq.shape, BF16) \
        + _nbytes(keys.shape, BF16) + _nbytes((tm, D), BF16) + 3 * _nbytes((PEER_SEL, tm), F32)
    temps = 5 * _nbytes((tm, QW), F32) + 16 * _nbytes((2 * PEER_SEL, tm), F32)
    return pl.pallas_call(
        _mix_out_kernel,
        grid=(T // tm,),
        in_specs=[pl.BlockSpec((oa4.shape[0], tm, V7X_LANES), lambda i: (0, i, 0)),
                  pl.BlockSpec((None, B_WIDTH, tm), lambda i: (i // nt, 0, i % nt)), row(D), full(wo),
                  pl.BlockSpec((1, D), lambda i: (0, 0)), full(wpq), full(keys)],
        out_specs=[row(D), row(D), row(PEER_SEL), row(PEER_SEL), row(PEER_SEL)],
        out_shape=[jax.ShapeDtypeStruct((T, D), F32), jax.ShapeDtypeStruct((T, D), BF16),
                   jax.ShapeDtypeStruct((T, PEER_SEL), I32), jax.ShapeDtypeStruct((T, PEER_SEL), I32),
                   jax.ShapeDtypeStruct((T, PEER_SEL), F32)],
        compiler_params=pltpu.CompilerParams(
            dimension_semantics=("parallel",), vmem_limit_bytes=_vmem_limit(blocks, temps)),
        name="mix_out_topk",
    )(oa4, omt, x2, wo, g_ffn.reshape(1, D), wpq, keys)


def _topk_rows(sc, k, order=None, payload=None):
    if order is None:
        order = lax.broadcasted_iota(I32, sc.shape, 0).astype(F32)
    vals, sel = [], []
    for i in range(k):
        m = jnp.max(sc, axis=0, keepdims=True)
        first = jnp.min(jnp.where(sc == m, order, _ORDER_NONE), axis=0, keepdims=True)
        hit = order == first
        if payload is None:
            got = first
        else:
            got = jnp.max(jnp.where(hit, payload, -1.0), axis=0, keepdims=True)
        vals.append(m)
        sel.append(got)
        sc = jnp.where(hit, -jnp.inf, sc)
    return jnp.concatenate(vals, axis=0), jnp.concatenate(sel, axis=0)


_ORDER_NONE = float(1 << 20)


def _candidate_pieces(k):
    pieces = []
    split = 0
    while (split + 1) * (split + 1) <= k:
        split += 1
    for r in range(split):
        n = k // (r + 1)
        for s0 in range(0, n, V7X_SUBLANES):
            pieces.append((True, r, s0, 0, min(V7X_SUBLANES, n - s0)))
    for s in range(split):
        n = k // (s + 1)
        for r0 in range(0, n, V7X_SUBLANES):
            lo, hi = max(split - r0, 0), min(V7X_SUBLANES, n - r0)
            if hi > lo:
                pieces.append((False, s, r0, lo, hi))
    return pieces


def _peer_topk(project, keys_ref):
    K = PEER_TOPK
    pieces = _candidate_pieces(K)
    q_next = project(0)
    tt = q_next.shape[0]
    j8 = lax.broadcasted_iota(I32, (V7X_SUBLANES, tt), 0)
    j8f = j8.astype(F32)
    i1s, i2s, gates = [], [], []
    for h in range(PEER_HEADS):
        q_head = q_next
        if h + 1 < PEER_HEADS:
            q_next = project(h + 1)
        vals, idxs = [], []
        for p in range(2):
            q = q_head[:, p * (PEER_QDIM // 2):(p + 1) * (PEER_QDIM // 2)].astype(BF16)
            sc = _dot_nt(keys_ref[2 * h + p], q)
            v, i = _topk_rows(sc, K)
            vals.append(v)
            idxs.append(i)
        cand, order, pair = [], [], []
        for n, (along_s, fixed, start, lo, hi) in enumerate(pieces):
            a, b = (0, 1) if along_s else (1, 0)
            sl = slice(start, start + V7X_SUBLANES)
            valid = (j8 >= lo) & (j8 < hi)
            cand.append(jnp.where(valid, vals[a][fixed:fixed + 1, :] + vals[b][sl, :], -jnp.inf))
            flat = (fixed * K + start + j8f) if along_s else ((start + j8f) * K + fixed)
            order.append(jnp.where(valid, flat, _ORDER_NONE - 1 - n * V7X_SUBLANES - j8f))
            e1 = idxs[0][fixed:fixed + 1, :] if along_s else idxs[0][sl, :]
            e2 = idxs[1][sl, :] if along_s else idxs[1][fixed:fixed + 1, :]
            pair.append(e1 * PEER_NKEYS + e2)
        best, expert = _topk_rows(jnp.concatenate(cand, axis=0), K,
                                  order=jnp.concatenate(order, axis=0),
                                  payload=jnp.concatenate(pair, axis=0))
        e = jnp.exp(best - jnp.max(best, axis=0, keepdims=True))
        gates.append(e / jnp.sum(e, axis=0, keepdims=True))
        expert = expert.astype(I32)
        i1s.append(expert // PEER_NKEYS)
        i2s.append(expert % PEER_NKEYS)
    return tuple(jnp.concatenate(parts, axis=0) for parts in (i1s, i2s, gates))


def _peer_act_kernel(hn_ref, u_ref, i1_ref, i2_ref, gate_ref, coef_ref, z_ref, *, chunks):
    g = pl.program_id(1)

    @pl.when(g == 0)
    def _():
        z_ref[...] = jnp.zeros_like(z_ref)

    hn = hn_ref[...]
    i1 = i1_ref[...]
    i2 = i2_ref[...]
    zsel = z_ref[...]
    for k in range(chunks // 2):
        z = _dot_nt(hn, u_ref[2 * k * PEER_NKEYS:(2 * k + 2) * PEER_NKEYS, :])
        for j in range(2):
            picked = jnp.take_along_axis(z[:, j * PEER_NKEYS:(j + 1) * PEER_NKEYS], i2, axis=1)
            zsel = jnp.where(i1 == g * chunks + 2 * k + j, picked, zsel)
    z_ref[...] = zsel

    @pl.when(g == pl.num_programs(1) - 1)
    def _():
        act = 0.5 * zsel * (1.0 + lax.erf(zsel * np.float32(math.sqrt(0.5))))
        coef_ref[...] = act * gate_ref[...]


def _peer_act(hn, u_bf, i1, i2, gate):
    T, D = hn.shape
    E = u_bf.shape[0]
    tm = min(1024, T)
    chunks = 32
    rows = chunks * PEER_NKEYS
    sel = pl.BlockSpec((tm, PEER_SEL), lambda i, g: (i, 0))
    blocks = _nbytes((tm, D), BF16) + _nbytes((rows, D), BF16) + 4 * _nbytes((tm, PEER_SEL), F32)
    return pl.pallas_call(
        functools.partial(_peer_act_kernel, chunks=chunks),
        grid=(T // tm, E // rows),
        in_specs=[pl.BlockSpec((tm, D), lambda i, g: (i, 0)),
                  pl.BlockSpec((rows, D), lambda i, g: (g, 0)), sel, sel, sel],
        out_specs=sel,
        out_shape=jax.ShapeDtypeStruct((T, PEER_SEL), F32),
        scratch_shapes=[pltpu.VMEM((tm, PEER_SEL), F32)],
        compiler_params=pltpu.CompilerParams(
            dimension_semantics=("parallel", "arbitrary"),
            vmem_limit_bytes=_vmem_limit(
                blocks, (chunks // 2 + 4) * _nbytes((tm, 2 * PEER_NKEYS), F32))),
        name="peer_act",
    )(hn, u_bf, i1, i2, gate)


def _peer_out_kernel(i1_ref, i2_ref, coef_ref, v_ref, h_ref, gfin_ref, out_ref,
                     sx_ref, acc_ref, *, tt, pitch, chunks):
    NK = PEER_NKEYS
    sub = lax.broadcasted_iota(I32, (NK, PEER_SEL), 0)

    def scatter_token(t, carry):
        i1r = i1_ref[pl.ds(t, 1), :]
        i2r = i2_ref[pl.ds(t, 1), :]
        cr = coef_ref[pl.ds(t, 1), :]
        a_t = jnp.where(sub == i1r, 1.0, 0.0).astype(BF16)
        b_t = jnp.where(sub == i2r, cr, 0.0).astype(BF16)
        s = _dot_nt(a_t, b_t)
        sx_ref[pl.ds(pl.multiple_of(t * pitch, V7X_SUBLANES), NK), :] = s
        return carry

    lax.fori_loop(0, tt, scatter_token, 0, unroll=SCATTER_UNROLL)

    acc_ref[...] = jnp.zeros_like(acc_ref)

    def chunk_group(g, carry):
        part = jnp.zeros(acc_ref.shape, F32)
        for cc in range(chunks // 2):
            c0 = g * chunks + 2 * cc
            lhs = jnp.concatenate(
                [sx_ref[pl.ds(c0 + k, tt, stride=pitch), :] for k in range(2)],
                axis=1).astype(BF16)
            rhs = v_ref[pl.ds(pl.multiple_of(c0 * NK, 2 * NK), 2 * NK), :]
            part = part + _dot(lhs, rhs)
        acc_ref[...] += part
        return carry

    lax.fori_loop(0, v_ref.shape[0] // (chunks * NK), chunk_group, 0)
    out_ref[...] = _rms(h_ref[...] + acc_ref[...], gfin_ref[...])


def _peer_out(i1, i2, coef, v_bf, h, g_final):
    T, D = h.shape
    E = v_bf.shape[0]
    tt = min(256, T)
    chunks = 16
    pitch = PEER_NKEYS + V7X_SUBLANES
    sel = pl.BlockSpec((tt, PEER_SEL), lambda i: (i, 0))
    tok = pl.BlockSpec((tt, D), lambda i: (i, 0))
    blocks = 3 * _nbytes((tt, PEER_SEL), F32) + 2 * _nbytes((tt, D), F32)
    scratch = _nbytes((tt * pitch, PEER_NKEYS), F32) + _nbytes((tt, D), F32)
    resident = _nbytes((E, D), BF16)
    return pl.pallas_call(
        functools.partial(_peer_out_kernel, tt=tt, pitch=pitch, chunks=chunks),
        grid=(T // tt,),
        in_specs=[sel, sel, sel,
                  pl.BlockSpec((E, D), lambda i: (0, 0), pipeline_mode=pl.Buffered(1)), tok,
                  pl.BlockSpec((1, D), lambda i: (0, 0))],
        out_specs=tok,
        out_shape=jax.ShapeDtypeStruct((T, D), F32),
        scratch_shapes=[pltpu.VMEM((tt * pitch, PEER_NKEYS), F32),
                        pltpu.VMEM((tt, D), F32)],
        compiler_params=pltpu.CompilerParams(
            dimension_semantics=("parallel",),
            vmem_limit_bytes=_vmem_limit(blocks, resident + scratch + 2 * _nbytes((tt, D), F32))),
        name="peer_out",
    )(i1, i2, coef, v_bf, h, g_final.reshape(1, D))


def kernel(x, positions, g_mix, w_in, g_cq, w_uq, g_ckv, w_ukv, w_out, g_ffn,
           w_pq, sub_keys, u_emb, v_emb, g_final):
    B, S, D = x.shape
    T = B * S
    assert g_mix.shape[0] == 1, "single-layer configuration"
    x2 = x.reshape(T, D)
    pos2 = positions.reshape(T, 1)

    qa, ka, va, qm, km, vmt = _mix_in(x2, pos2, g_mix[0], w_in[0], g_cq[0], w_uq[0],
                                      g_ckv[0], w_ukv[0], tm=min(MLA_TK, S))
    oa4 = _dilated(qa, ka, va, B, S)
    omt = _mla(qm, km, vmt, B, S)
    h, hn, i1, i2, gate = _mix_out(oa4, omt, x2, w_out[0], g_ffn[0], w_pq[0], sub_keys[0])
    coef = _peer_act(hn, u_emb[0].astype(BF16), i1, i2, gate)
    out = _peer_out(i1, i2, coef, v_emb[0].astype(BF16), h, g_final)
    return out.reshape(B, S, D)
```

```python
import functools
import math

import jax
import jax.numpy as jnp
import numpy as np
from jax import lax
from jax.experimental import pallas as pl
from jax.experimental.pallas import tpu as pltpu

F32 = jnp.float32
BF16 = jnp.bfloat16
I32 = jnp.int32

ROPE_THETA = 500000.0
NORM_EPS = 1e-6
NEG_INF = -1e30
BLK = 128
A_HEADS = 8
HEAD_DIM = 64
A_ROT_DIM = HEAD_DIM // 4
DIL_PATTERNS = ((128, 1), (512, 4), (2048, 16))
A_WIDTH = A_HEADS * HEAD_DIM
B_HEADS = 8
MLA_Q_RANK = 256
MLA_KV_RANK = 128
MLA_NOPE = 64
MLA_ROPE = 32
MLA_V = 64
B_WIDTH = B_HEADS * MLA_V
PEER_HEADS = 8
PEER_NKEYS = 128
PEER_QDIM = 256
PEER_TOPK = 16
PEER_SEL = PEER_HEADS * PEER_TOPK
SCATTER_UNROLL = 64
MLA_TQ = 512
MLA_TK = 512

V7X_LANES = 128
V7X_SUBLANES = 8
V7X_VMEM_BYTES = 64 * 1024 * 1024
V7X_VMEM_REQUEST_CAP = 56 * 1024 * 1024


def _vmem_limit(pipelined_bytes, resident_bytes=0):
    est = 2 * int(pipelined_bytes) + int(resident_bytes)
    return int(min(max(est, 16 * 1024 * 1024), V7X_VMEM_REQUEST_CAP))


def _nbytes(shape, dtype):
    return int(np.prod(shape)) * jnp.dtype(dtype).itemsize


def _dot(a, b):
    return jnp.dot(a, b, preferred_element_type=F32)


def _dot_nt(a, b):
    return lax.dot_general(a, b, (((1,), (1,)), ((), ())), preferred_element_type=F32)


def _rms(x, g):
    ms = jnp.mean(x * x, axis=-1, keepdims=True)
    return x * lax.rsqrt(ms + NORM_EPS) * g


def _rope_tables(cos, sin, lo_start, half, period):
    lane = lax.broadcasted_iota(I32, (1, V7X_LANES), 1) & (period - 1)
    is_lo = (lane >= lo_start) & (lane < lo_start + half)
    is_hi = (lane >= lo_start + half) & (lane < lo_start + 2 * half)
    c = jnp.where(is_lo | is_hi, cos, 1.0)
    s_lo = jnp.where(is_lo, -sin, 0.0)
    s_hi = jnp.where(is_hi, sin, 0.0)
    return c, s_lo, s_hi


def _apply_rope(x, tables, half):
    c, s_lo, s_hi = tables
    return (x * c + pltpu.roll(x, V7X_LANES - half, 1) * s_lo
            + pltpu.roll(x, half, 1) * s_hi)


def _mix_in_kernel(x_ref, pos_ref, gmix_ref, wa_ref, wcq_ref, wckv_ref, wkr_ref,
                   gcq_ref, wuq_ref, gckv_ref, wuk_ref, wuvt_ref, inv_ref,
                   qa_ref, ka_ref, va_ref, qm_ref, km_ref, vmt_ref, *, a_scale, m_scale):
    hn = _rms(x_ref[...], gmix_ref[...]).astype(BF16)
    ang = pos_ref[...].astype(F32) * inv_ref[...]
    cos, sin = jnp.cos(ang), jnp.sin(ang)
    first_head = lax.broadcasted_iota(I32, (1, V7X_LANES), 1) < HEAD_DIM
    cos_a = jnp.where(first_head, cos, pltpu.roll(cos, HEAD_DIM, 1))
    sin_a = jnp.where(first_head, sin, pltpu.roll(sin, HEAD_DIM, 1))
    tab_a = _rope_tables(cos_a, sin_a, 0, A_ROT_DIM // 2, HEAD_DIM)
    tab_m = _rope_tables(cos, sin, MLA_NOPE, MLA_ROPE // 2, V7X_LANES)

    proj = _dot(hn, wa_ref[...])
    for c in range(A_WIDTH // V7X_LANES):
        sl = slice(c * V7X_LANES, (c + 1) * V7X_LANES)
        q = proj[:, sl]
        k = proj[:, A_WIDTH + c * V7X_LANES:A_WIDTH + (c + 1) * V7X_LANES]
        qa_ref[c] = _apply_rope(q, tab_a, A_ROT_DIM // 2) * a_scale
        ka_ref[c] = _apply_rope(k, tab_a, A_ROT_DIM // 2)
        va_ref[c] = proj[:, 2 * A_WIDTH + c * V7X_LANES:2 * A_WIDTH + (c + 1) * V7X_LANES]

    cq = _rms(_dot(hn, wcq_ref[...]), gcq_ref[...]).astype(BF16)
    qm = _dot(cq, wuq_ref[...])
    ckv = _rms(_dot(hn, wckv_ref[...]), gckv_ref[...]).astype(BF16)
    kn = _dot(ckv, wuk_ref[...])
    kr = _apply_rope(_dot(hn, wkr_ref[...]), tab_m, MLA_ROPE // 2)
    for h in range(B_HEADS):
        sl = slice(h * V7X_LANES, (h + 1) * V7X_LANES)
        qm_ref[:, sl] = (_apply_rope(qm[:, sl], tab_m, MLA_ROPE // 2) * m_scale).astype(BF16)
        km_ref[:, sl] = (kn[:, sl] + kr).astype(BF16)
    vmt_ref[...] = _dot_nt(wuvt_ref[...], ckv).astype(BF16).reshape(vmt_ref.shape)


def _mix_in(x2, pos2, g_mix, w_in, g_cq, w_uq, g_ckv, w_ukv, tm):
    T, D = x2.shape
    nope_rope = MLA_NOPE + MLA_ROPE
    wa = w_in[:, :3 * A_WIDTH].astype(BF16)
    o0 = 3 * A_WIDTH
    wcq = w_in[:, o0:o0 + MLA_Q_RANK].astype(BF16)
    wckv = w_in[:, o0 + MLA_Q_RANK:o0 + MLA_Q_RANK + MLA_KV_RANK].astype(BF16)
    wkr_raw = w_in[:, o0 + MLA_Q_RANK + MLA_KV_RANK:]
    wkr = jnp.pad(wkr_raw, ((0, 0), (MLA_NOPE, V7X_LANES - nope_rope))).astype(BF16)
    wuq = jnp.pad(w_uq.reshape(MLA_Q_RANK, B_HEADS, nope_rope),
                  ((0, 0), (0, 0), (0, V7X_LANES - nope_rope))
                  ).reshape(MLA_Q_RANK, B_HEADS * V7X_LANES).astype(BF16)
    wukv = w_ukv.reshape(MLA_KV_RANK, B_HEADS, MLA_NOPE + MLA_V)
    wuk = jnp.pad(wukv[:, :, :MLA_NOPE], ((0, 0), (0, 0), (0, V7X_LANES - MLA_NOPE))
                  ).reshape(MLA_KV_RANK, B_HEADS * V7X_LANES).astype(BF16)
    wuvt = wukv[:, :, MLA_NOPE:].reshape(MLA_KV_RANK, B_WIDTH).T.astype(BF16)
    lane = jnp.arange(V7X_LANES)
    ha, hm = A_ROT_DIM // 2, MLA_ROPE // 2
    inv_a = (ROPE_THETA ** (-jnp.arange(ha, dtype=F32) / ha))[lane % ha]
    inv_m = (ROPE_THETA ** (-jnp.arange(hm, dtype=F32) / hm))[lane % hm]
    inv = jnp.where(lane < A_ROT_DIM, inv_a,
                    jnp.where((lane >= MLA_NOPE) & (lane < nope_rope), inv_m, 0.0)
                    ).reshape(1, V7X_LANES)

    full = lambda a: pl.BlockSpec(a.shape, lambda i: (0,) * a.ndim)
    row = lambda w: pl.BlockSpec((tm, w), lambda i: (i, 0))
    consts = (g_mix.reshape(1, D), wa, wcq, wckv, wkr, g_cq.reshape(1, -1), wuq,
              g_ckv.reshape(1, -1), wuk, wuvt, inv)
    a_shape = (A_WIDTH // V7X_LANES, tm, V7X_LANES)
    m_width = B_HEADS * V7X_LANES
    vt_shape = (B_HEADS // 2, 2 * MLA_V, tm)
    blocks = (_nbytes((tm, D), F32) + sum(_nbytes(c.shape, c.dtype) for c in consts)
              + 3 * _nbytes(a_shape, F32) + 2 * _nbytes((tm, m_width), BF16)
              + _nbytes(vt_shape, BF16))
    temps = _nbytes((tm, 3 * A_WIDTH + 4 * B_HEADS * V7X_LANES), F32)
    a_spec = pl.BlockSpec(a_shape, lambda i: (0, i, 0))
    a_out = jax.ShapeDtypeStruct((a_shape[0], T, V7X_LANES), F32)
    return pl.pallas_call(
        functools.partial(_mix_in_kernel, a_scale=HEAD_DIM ** -0.5,
                          m_scale=nope_rope ** -0.5 * math.log2(math.e)),
        grid=(T // tm,),
        in_specs=[row(D), pl.BlockSpec((tm, 1), lambda i: (i, 0))] + [full(c) for c in consts],
        out_specs=[a_spec] * 3 + [row(m_width)] * 2
        + [pl.BlockSpec((None,) + vt_shape, lambda i: (i, 0, 0, 0))],
        out_shape=[a_out] * 3 + [jax.ShapeDtypeStruct((T, m_width), BF16)] * 2
        + [jax.ShapeDtypeStruct((T // tm,) + vt_shape, BF16)],
        compiler_params=pltpu.CompilerParams(
            dimension_semantics=("parallel",), vmem_limit_bytes=_vmem_limit(blocks, temps)),
        name="mix_in",
    )(x2, pos2, *consts)


def _dilated_kernel(q_ref, kp_ref, kc_ref, vp_ref, vc_ref, o_ref, *scratch, patterns, rows):
    n = pl.program_id(2)
    o_scr, l_scr = scratch[:len(patterns)], scratch[len(patterns):]
    lane = lax.broadcasted_iota(I32, (1, V7X_LANES), 1)
    lo = lane < HEAD_DIM
    qi = lax.broadcasted_iota(I32, (BLK, 2 * BLK), 0)
    kj = lax.broadcasted_iota(I32, (BLK, 2 * BLK), 1)
    dist = qi + BLK - kj

    for pi, (span, dil) in enumerate(patterns):
        band = (dist >= 0) & (dist <= span)
        first_valid = band & (kj >= jnp.where(n > 0, 0, BLK))
        step = BLK * dil

        def take(ref, start, dil=dil):
            idx = pl.ds(start, BLK, stride=dil) if dil > 1 else pl.ds(start, BLK)
            return ref[idx, :]

        def block(r, j, pi=pi, dil=dil, step=step, take=take, band=band, first_valid=first_valid):
            start = j * step + r
            q2 = take(q_ref, start).astype(BF16)
            if j == 0:
                kprev, vprev = take(kp_ref, rows - step + r), take(vp_ref, rows - step + r)
                valid = first_valid
            else:
                kprev, vprev = take(kc_ref, start - step), take(vc_ref, start - step)
                valid = band
            k2 = jnp.concatenate([kprev, take(kc_ref, start)], axis=0).astype(BF16)
            v2 = jnp.concatenate([vprev, take(vc_ref, start)], axis=0).astype(BF16)
            acc = jnp.zeros((BLK, V7X_LANES), F32)
            lse2 = jnp.zeros((BLK, V7X_LANES), F32)
            for h in range(2):
                sel = lo if h == 0 else ~lo
                qh = jnp.where(sel, q2, jnp.zeros_like(q2))
                vh = jnp.where(sel, v2, jnp.zeros_like(v2))
                s = jnp.where(valid, _dot_nt(qh, k2), NEG_INF)
                m = jnp.max(s, axis=-1, keepdims=True)
                p = jnp.exp(s - m)
                den = jnp.sum(p, axis=-1, keepdims=True)
                acc = acc + _dot(p.astype(BF16), vh) / den
                lse2 = jnp.where(sel, m + jnp.log(den), lse2)
            idx = pl.ds(start, BLK, stride=dil) if dil > 1 else pl.ds(start, BLK)
            o_scr[pi][idx, :] = acc
            l_scr[pi][idx, :] = lse2

        for r in range(dil):
            for j in range(rows // step):
                block(r, j)

    lses = [l[...] for l in l_scr]
    mx = functools.reduce(jnp.maximum, lses)
    es = [jnp.exp(l - mx) for l in lses]
    num = sum(e * o[...] for e, o in zip(es, o_scr))
    o_ref[...] = (num / sum(es)).astype(o_ref.dtype)


def _dilated(qa, ka, va, B, S):
    patterns = tuple((w // d, d) for w, d in DIL_PATTERNS)
    rows = BLK * max(d for _, d in patterns)
    assert all(span <= BLK for span, _ in patterns) and S % rows == 0
    nb = S // rows
    npairs = qa.shape[0]
    cur = pl.BlockSpec((None, rows, V7X_LANES), lambda b, h, n: (h, b * nb + n, 0))
    prev = pl.BlockSpec((None, rows, V7X_LANES),
                        lambda b, h, n: (h, b * nb + jnp.maximum(n - 1, 0), 0))
    blocks = 5 * _nbytes((rows, V7X_LANES), F32) + _nbytes((rows, V7X_LANES), BF16)
    scratch = 2 * len(patterns) * _nbytes((rows, V7X_LANES), F32)
    return pl.pallas_call(
        functools.partial(_dilated_kernel, patterns=patterns, rows=rows),
        grid=(B, npairs, nb),
        in_specs=[cur, prev, cur, prev, cur],
        out_specs=cur,
        out_shape=jax.ShapeDtypeStruct(qa.shape, BF16),
        scratch_shapes=[pltpu.VMEM((rows, V7X_LANES), F32)] * (2 * len(patterns)),
        compiler_params=pltpu.CompilerParams(
            dimension_semantics=("parallel", "parallel", "arbitrary"),
            vmem_limit_bytes=_vmem_limit(blocks, scratch + 4 * 1024 * 1024)),
        name="dilated_attention",
    )(qa, ka, ka, va, va)


def _mla_kernel(q_ref, k_ref, vt_ref, o_ref, s_ref, *, tq, tk):
    qi = pl.program_id(2)
    qs = [q_ref[:, h * V7X_LANES:(h + 1) * V7X_LANES] for h in range(2)]
    q_start = qi * tq

    def scores_into(slot, ki):
        kb = k_ref[pl.ds(pl.multiple_of(ki * tk, tk), tk), :]
        for h in range(2):
            s_ref[slot, h] = _dot_nt(kb[:, h * V7X_LANES:(h + 1) * V7X_LANES], qs[h])

    def consume(slot, ki, carry, masked, refill=None):
        ms, ls, accs = carry
        vtb = vt_ref[ki]
        new_ms, new_ls, alphas, ps = [], [], [], []
        for h in range(2):
            s = s_ref[slot, h]
            if masked:
                kpos = ki * tk + lax.broadcasted_iota(I32, (tk, tq), 0)
                qpos = q_start + lax.broadcasted_iota(I32, (tk, tq), 1)
                s = jnp.where(kpos <= qpos, s, NEG_INF)
            m_new = jnp.maximum(ms[h], jnp.max(s, axis=0, keepdims=True))
            alphas.append(jnp.exp2(ms[h] - m_new))
            p = jnp.exp2(s - m_new)
            new_ls.append(alphas[h] * ls[h] + jnp.sum(p, axis=0, keepdims=True))
            new_ms.append(m_new)
            ps.append(p.astype(BF16))
        if refill is not None:
            scores_into(slot, refill)
        new_accs = [accs[h] * alphas[h] + _dot(vtb[h * MLA_V:(h + 1) * MLA_V, :], ps[h])
                    for h in range(2)]
        return tuple(new_ms), tuple(new_ls), tuple(new_accs)

    def pair(j, carry):
        scores_into(1, 2 * j + 1)
        carry = consume(0, 2 * j, carry, False, refill=2 * j + 2)
        return consume(1, 2 * j + 1, carry, False)

    def even_tail(carry):
        return consume(0, n_full, carry, True)

    def odd_tail(carry):
        scores_into(1, n_full)
        carry = consume(0, n_full - 1, carry, False)
        return consume(1, n_full, carry, True)

    init = ((jnp.full((1, tq), NEG_INF, F32),) * 2, (jnp.zeros((1, tq), F32),) * 2,
            (jnp.zeros((MLA_V, tq), F32),) * 2)
    n_full = q_start // tk
    scores_into(0, 0)
    carry = lax.fori_loop(0, n_full // 2, pair, init)
    if tq == tk:
        _, ls, accs = lax.cond(n_full % 2 == 0, even_tail, odd_tail, carry)
    else:
        scores_into(1, n_full + 1)
        carry = consume(0, n_full, carry, True)
        _, ls, accs = consume(1, n_full + 1, carry, True)
    for h in range(2):
        o_ref[h * MLA_V:(h + 1) * MLA_V, :] = (accs[h] / ls[h]).astype(o_ref.dtype)


def _mla(qm, km, vmt, B, S):
    tk = vmt.shape[-1]
    tq = min(MLA_TQ, S)
    assert tq in (tk, 2 * tk) and S % tq == 0
    nk = S // tk
    hw = 2 * V7X_LANES
    q3, k3 = (a.reshape(B, S, -1) for a in (qm, km))
    blocks = _nbytes((tq, hw), BF16) + _nbytes((S, hw), BF16) + _nbytes((S, V7X_LANES), BF16) \
        + _nbytes((tq, V7X_LANES), BF16)
    score_scratch = (2, 2, tk, tq)
    temps = _nbytes(score_scratch, F32) + 6 * _nbytes((tk, tq), F32)
    return pl.pallas_call(
        functools.partial(_mla_kernel, tq=tq, tk=tk),
        grid=(B, B_HEADS // 2, S // tq),
        in_specs=[pl.BlockSpec((None, tq, hw), lambda b, h, i: (b, i, h)),
                  pl.BlockSpec((None, S, hw), lambda b, h, i: (b, 0, h)),
                  pl.BlockSpec((nk, None, V7X_LANES, tk), lambda b, h, i: (b, h, 0, 0))],
        out_specs=pl.BlockSpec((None, V7X_LANES, tq), lambda b, h, i: (b, h, i)),
        out_shape=jax.ShapeDtypeStruct((B, B_WIDTH, S), BF16),
        scratch_shapes=[pltpu.VMEM(score_scratch, F32)],
        compiler_params=pltpu.CompilerParams(
            dimension_semantics=("parallel", "parallel", "arbitrary"),
            vmem_limit_bytes=_vmem_limit(blocks, temps)),
        name="mla_attention",
    )(q3, k3, vmt)


def _mix_out_kernel(oa_ref, om_ref, x_ref, wo_ref, gffn_ref, wpq_ref, keys_ref,
                    h_ref, hn_ref, i1_ref, i2_ref, gate_ref):
    oa = jnp.concatenate([oa_ref[c] for c in range(oa_ref.shape[0])], axis=1)
    mixed_m = lax.dot_general(om_ref[...], wo_ref[A_WIDTH:, :], (((0,), (0,)), ((), ())),
                              preferred_element_type=F32)
    h = x_ref[...] + _dot(oa, wo_ref[:A_WIDTH, :]) + mixed_m
    h_ref[...] = h
    hn = _rms(h, gffn_ref[...]).astype(BF16)
    hn_ref[...] = hn
    i1, i2, gate = _peer_topk(
        lambda head: _dot(hn, wpq_ref[:, head * PEER_QDIM:(head + 1) * PEER_QDIM]), keys_ref)
    i1_ref[...], i2_ref[...], gate_ref[...] = i1.T, i2.T, gate.T


def _mix_out(oa4, omt, x2, w_out, g_ffn, w_pq, sub_keys):
    T, D = x2.shape
    S = omt.shape[-1]
    tm = min(256, S)
    nt = S // tm
    wo = w_out.astype(BF16)
    wpq = w_pq.astype(BF16)
    keys = sub_keys.reshape(PEER_HEADS * 2, PEER_NKEYS, PEER_QDIM // 2).astype(BF16)
    QW = wpq.shape[1]
    row = lambda w: pl.BlockSpec((tm, w), lambda i: (i, 0))
    full = lambda a: pl.BlockSpec(a.shape, lambda i: (0,) * a.ndim)
    blocks = _nbytes((tm, A_WIDTH), BF16) + _nbytes((tm, B_WIDTH), BF16) \
        + 2 * _nbytes((tm, D), F32) + _nbytes(wo.shape, BF16) + _nbytes(wpq.shape, BF16) \
        + _nbytes(keys.shape, BF16) + _nbytes((tm, D), BF16) + 3 * _nbytes((PEER_SEL, tm), F32)
    temps = 5 * _nbytes((tm, QW), F32) + 16 * _nbytes((2 * PEER_SEL, tm), F32)
    return pl.pallas_call(
        _mix_out_kernel,
        grid=(T // tm,),
        in_specs=[pl.BlockSpec((oa4.shape[0], tm, V7X_LANES), lambda i: (0, i, 0)),
                  pl.BlockSpec((None, B_WIDTH, tm), lambda i: (i // nt, 0, i % nt)), row(D), full(wo),
                  pl.BlockSpec((1, D), lambda i: (0, 0)), full(wpq), full(keys)],
        out_specs=[row(D), row(D), row(PEER_SEL), row(PEER_SEL), row(PEER_SEL)],
        out_shape=[jax.ShapeDtypeStruct((T, D), F32), jax.ShapeDtypeStruct((T, D), BF16),
                   jax.ShapeDtypeStruct((T, PEER_SEL), I32), jax.ShapeDtypeStruct((T, PEER_SEL), I32),
                   jax.ShapeDtypeStruct((T, PEER_SEL), F32)],
        compiler_params=pltpu.CompilerParams(
            dimension_semantics=("parallel",), vmem_limit_bytes=_vmem_limit(blocks, temps)),
        name="mix_out_topk",
    )(oa4, omt, x2, wo, g_ffn.reshape(1, D), wpq, keys)


def _topk_rows(sc, k, order=None, payload=None):
    if order is None:
        order = lax.broadcasted_iota(I32, sc.shape, 0).astype(F32)
    vals, sel = [], []
    for i in range(k):
        m = jnp.max(sc, axis=0, keepdims=True)
        first = jnp.min(jnp.where(sc == m, order, _ORDER_NONE), axis=0, keepdims=True)
        hit = order == first
        if payload is None:
            got = first
        else:
            got = jnp.max(jnp.where(hit, payload, -1.0), axis=0, keepdims=True)
        vals.append(m)
        sel.append(got)
        sc = jnp.where(hit, -jnp.inf, sc)
    return jnp.concatenate(vals, axis=0), jnp.concatenate(sel, axis=0)


_ORDER_NONE = float(1 << 20)


def _candidate_pieces(k):
    pieces = []
    split = 0
    while (split + 1) * (split + 1) <= k:
        split += 1
    for r in range(split):
        n = k // (r + 1)
        for s0 in range(0, n, V7X_SUBLANES):
            pieces.append((True, r, s0, 0, min(V7X_SUBLANES, n - s0)))
    for s in range(split):
        n = k // (s + 1)
        for r0 in range(0, n, V7X_SUBLANES):
            lo, hi = max(split - r0, 0), min(V7X_SUBLANES, n - r0)
            if hi > lo:
                pieces.append((False, s, r0, lo, hi))
    return pieces


def _peer_topk(project, keys_ref):
    K = PEER_TOPK
    pieces = _candidate_pieces(K)
    q_next = project(0)
    tt = q_next.shape[0]
    j8 = lax.broadcasted_iota(I32, (V7X_SUBLANES, tt), 0)
    j8f = j8.astype(F32)
    i1s, i2s, gates = [], [], []
    for h in range(PEER_HEADS):
        q_head = q_next
        if h + 1 < PEER_HEADS:
            q_next = project(h + 1)
        vals, idxs = [], []
        for p in range(2):
            q = q_head[:, p * (PEER_QDIM // 2):(p + 1) * (PEER_QDIM // 2)].astype(BF16)
            sc = _dot_nt(keys_ref[2 * h + p], q)
            v, i = _topk_rows(sc, K)
            vals.append(v)
            idxs.append(i)
        cand, order, pair = [], [], []
        for n, (along_s, fixed, start, lo, hi) in enumerate(pieces):
            a, b = (0, 1) if along_s else (1, 0)
            sl = slice(start, start + V7X_SUBLANES)
            valid = (j8 >= lo) & (j8 < hi)
            cand.append(jnp.where(valid, vals[a][fixed:fixed + 1, :] + vals[b][sl, :], -jnp.inf))
            flat = (fixed * K + start + j8f) if along_s else ((start + j8f) * K + fixed)
            order.append(jnp.where(valid, flat, _ORDER_NONE - 1 - n * V7X_SUBLANES - j8f))
            e1 = idxs[0][fixed:fixed + 1, :] if along_s else idxs[0][sl, :]
            e2 = idxs[1][sl, :] if along_s else idxs[1][fixed:fixed + 1, :]
            pair.append(e1 * PEER_NKEYS + e2)
        best, expert = _topk_rows(jnp.concatenate(cand, axis=0), K,
                                  order=jnp.concatenate(order, axis=0),
                                  payload=jnp.concatenate(pair, axis=0))
        e = jnp.exp(best - jnp.max(best, axis=0, keepdims=True))
        gates.append(e / jnp.sum(e, axis=0, keepdims=True))
        expert = expert.astype(I32)
        i1s.append(expert // PEER_NKEYS)
        i2s.append(expert % PEER_NKEYS)
    return tuple(jnp.concatenate(parts, axis=0) for parts in (i1s, i2s, gates))


def _peer_act_kernel(hn_ref, u_ref, i1_ref, i2_ref, gate_ref, coef_ref, z_ref, *, chunks):
    g = pl.program_id(1)

    @pl.when(g == 0)
    def _():
        z_ref[...] = jnp.zeros_like(z_ref)

    hn = hn_ref[...]
    i1 = i1_ref[...]
    i2 = i2_ref[...]
    zsel = z_ref[...]
    for k in range(chunks // 2):
        z = _dot_nt(hn, u_ref[2 * k * PEER_NKEYS:(2 * k + 2) * PEER_NKEYS, :])
        for j in range(2):
            picked = jnp.take_along_axis(z[:, j * PEER_NKEYS:(j + 1) * PEER_NKEYS], i2, axis=1)
            zsel = jnp.where(i1 == g * chunks + 2 * k + j, picked, zsel)
    z_ref[...] = zsel

    @pl.when(g == pl.num_programs(1) - 1)
    def _():
        act = 0.5 * zsel * (1.0 + lax.erf(zsel * np.float32(math.sqrt(0.5))))
        coef_ref[...] = act * gate_ref[...]


def _peer_act(hn, u_bf, i1, i2, gate):
    T, D = hn.shape
    E = u_bf.shape[0]
    tm = min(1024, T)
    chunks = 32
    rows = chunks * PEER_NKEYS
    sel = pl.BlockSpec((tm, PEER_SEL), lambda i, g: (i, 0))
    blocks = _nbytes((tm, D), BF16) + _nbytes((rows, D), BF16) + 4 * _nbytes((tm, PEER_SEL), F32)
    return pl.pallas_call(
        functools.partial(_peer_act_kernel, chunks=chunks),
        grid=(T // tm, E // rows),
        in_specs=[pl.BlockSpec((tm, D), lambda i, g: (i, 0)),
                  pl.BlockSpec((rows, D), lambda i, g: (g, 0)), sel, sel, sel],
        out_specs=sel,
        out_shape=jax.ShapeDtypeStruct((T, PEER_SEL), F32),
        scratch_shapes=[pltpu.VMEM((tm, PEER_SEL), F32)],
        compiler_params=pltpu.CompilerParams(
            dimension_semantics=("parallel", "arbitrary"),
            vmem_limit_bytes=_vmem_limit(
                blocks, (chunks // 2 + 4) * _nbytes((tm, 2 * PEER_NKEYS), F32))),
        name="peer_act",
    )(hn, u_bf, i1, i2, gate)


def _peer_out_kernel(i1_ref, i2_ref, coef_ref, v_ref, h_ref, gfin_ref, out_ref,
                     sx_ref, acc_ref, *, tt, pitch, chunks):
    NK = PEER_NKEYS
    sub = lax.broadcasted_iota(I32, (NK, PEER_SEL), 0)

    def scatter_token(t, carry):
        i1r = i1_ref[pl.ds(t, 1), :]
        i2r = i2_ref[pl.ds(t, 1), :]
        cr = coef_ref[pl.ds(t, 1), :]
        a_t = jnp.where(sub == i1r, 1.0, 0.0).astype(BF16)
        b_t = jnp.where(sub == i2r, cr, 0.0).astype(BF16)
        s = _dot_nt(a_t, b_t)
        sx_ref[pl.ds(t * pitch, NK), :] = s
        return carry

    lax.fori_loop(0, tt, scatter_token, 0, unroll=SCATTER_UNROLL)

    acc_ref[...] = jnp.zeros_like(acc_ref)

    def chunk_group(g, carry):
        part = jnp.zeros(acc_ref.shape, F32)
        for cc in range(chunks // 2):
            c0 = g * chunks + 2 * cc
            lhs = jnp.concatenate(
                [sx_ref[pl.ds(c0 + k, tt, stride=pitch), :] for k in range(2)],
                axis=1).astype(BF16)
            rhs = v_ref[pl.ds(pl.multiple_of(c0 * NK, 2 * NK), 2 * NK), :]
            part = part + _dot(lhs, rhs)
        acc_ref[...] += part
        return carry

    lax.fori_loop(0, v_ref.shape[0] // (chunks * NK), chunk_group, 0)
    out_ref[...] = _rms(h_ref[...] + acc_ref[...], gfin_ref[...])


def _peer_out(i1, i2, coef, v_bf, h, g_final):
    T, D = h.shape
    E = v_bf.shape[0]
    tt = min(256, T)
    chunks = 16
    pitch = PEER_NKEYS + 4
    sel = pl.BlockSpec((tt, PEER_SEL), lambda i: (i, 0))
    tok = pl.BlockSpec((tt, D), lambda i: (i, 0))
    blocks = 3 * _nbytes((tt, PEER_SEL), F32) + 2 * _nbytes((tt, D), F32)
    scratch = _nbytes((tt * pitch, PEER_NKEYS), F32) + _nbytes((tt, D), F32)
    resident = _nbytes((E, D), BF16)
    return pl.pallas_call(
        functools.partial(_peer_out_kernel, tt=tt, pitch=pitch, chunks=chunks),
        grid=(T // tt,),
        in_specs=[sel, sel, sel,
                  pl.BlockSpec((E, D), lambda i: (0, 0), pipeline_mode=pl.Buffered(1)), tok,
                  pl.BlockSpec((1, D), lambda i: (0, 0))],
        out_specs=tok,
        out_shape=jax.ShapeDtypeStruct((T, D), F32),
        scratch_shapes=[pltpu.VMEM((tt * pitch, PEER_NKEYS), F32),
                        pltpu.VMEM((tt, D), F32)],
        compiler_params=pltpu.CompilerParams(
            dimension_semantics=("parallel",),
            vmem_limit_bytes=_vmem_limit(blocks, resident + scratch + 2 * _nbytes((tt, D), F32))),
        name="peer_out",
    )(i1, i2, coef, v_bf, h, g_final.reshape(1, D))


def kernel(x, positions, g_mix, w_in, g_cq, w_uq, g_ckv, w_ukv, w_out, g_ffn,
           w_pq, sub_keys, u_emb, v_emb, g_final):
    B, S, D = x.shape
    T = B * S
    assert g_mix.shape[0] == 1, "single-layer configuration"
    x2 = x.reshape(T, D)
    pos2 = positions.reshape(T, 1)

    qa, ka, va, qm, km, vmt = _mix_in(x2, pos2, g_mix[0], w_in[0], g_cq[0], w_uq[0],
                                      g_ckv[0], w_ukv[0], tm=min(MLA_TK, S))
    oa4 = _dilated(qa, ka, va, B, S)
    omt = _mla(qm, km, vmt, B, S)
    h, hn, i1, i2, gate = _mix_out(oa4, omt, x2, w_out[0], g_ffn[0], w_pq[0], sub_keys[0])
    coef = _peer_act(hn, u_emb[0].astype(BF16), i1, i2, gate)
    out = _peer_out(i1, i2, coef, v_emb[0].astype(BF16), h, g_final)
    return out.reshape(B, S, D)
```

```python
import functools
import math

import jax
import jax.numpy as jnp
import numpy as np
from jax import lax
from jax.experimental import pallas as pl
from jax.experimental.pallas import tpu as pltpu

F32 = jnp.float32
BF16 = jnp.bfloat16
I32 = jnp.int32

ROPE_THETA = 500000.0
NORM_EPS = 1e-6
NEG_INF = -1e30
BLK = 128
A_HEADS = 8
HEAD_DIM = 64
A_ROT_DIM = HEAD_DIM // 4
DIL_PATTERNS = ((128, 1), (512, 4), (2048, 16))
A_WIDTH = A_HEADS * HEAD_DIM
B_HEADS = 8
MLA_Q_RANK = 256
MLA_KV_RANK = 128
MLA_NOPE = 64
MLA_ROPE = 32
MLA_V = 64
B_WIDTH = B_HEADS * MLA_V
PEER_HEADS = 8
PEER_NKEYS = 128
PEER_QDIM = 256
PEER_TOPK = 16
PEER_SEL = PEER_HEADS * PEER_TOPK

MLA_TQ = 512
MLA_TK = 512

V7X_LANES = 128
V7X_SUBLANES = 8
V7X_VMEM_BYTES = 64 * 1024 * 1024
V7X_VMEM_REQUEST_CAP = 56 * 1024 * 1024


def _vmem_limit(pipelined_bytes, resident_bytes=0):
    est = 2 * int(pipelined_bytes) + int(resident_bytes)
    return int(min(max(est, 16 * 1024 * 1024), V7X_VMEM_REQUEST_CAP))


def _nbytes(shape, dtype):
    return int(np.prod(shape)) * jnp.dtype(dtype).itemsize


def _dot(a, b):
    return jnp.dot(a, b, preferred_element_type=F32)


def _dot_nt(a, b):
    return lax.dot_general(a, b, (((1,), (1,)), ((), ())), preferred_element_type=F32)


def _rms(x, g):
    ms = jnp.mean(x * x, axis=-1, keepdims=True)
    return x * lax.rsqrt(ms + NORM_EPS) * g


def _rope_tables(cos, sin, lo_start, half, period):
    lane = lax.broadcasted_iota(I32, (1, V7X_LANES), 1) & (period - 1)
    is_lo = (lane >= lo_start) & (lane < lo_start + half)
    is_hi = (lane >= lo_start + half) & (lane < lo_start + 2 * half)
    c = jnp.where(is_lo | is_hi, cos, 1.0)
    s_lo = jnp.where(is_lo, -sin, 0.0)
    s_hi = jnp.where(is_hi, sin, 0.0)
    return c, s_lo, s_hi


def _apply_rope(x, tables, half):
    c, s_lo, s_hi = tables
    return (x * c + pltpu.roll(x, V7X_LANES - half, 1) * s_lo
            + pltpu.roll(x, half, 1) * s_hi)


def _mix_in_kernel(x_ref, pos_ref, gmix_ref, wa_ref, wcq_ref, wckv_ref, wkr_ref,
                   gcq_ref, wuq_ref, gckv_ref, wuk_ref, wuvt_ref, inv_ref,
                   qa_ref, ka_ref, va_ref, qm_ref, km_ref, vmt_ref, *, a_scale, m_scale):
    hn = _rms(x_ref[...], gmix_ref[...]).astype(BF16)
    ang = pos_ref[...].astype(F32) * inv_ref[...]
    cos, sin = jnp.cos(ang), jnp.sin(ang)
    first_head = lax.broadcasted_iota(I32, (1, V7X_LANES), 1) < HEAD_DIM
    cos_a = jnp.where(first_head, cos, pltpu.roll(cos, HEAD_DIM, 1))
    sin_a = jnp.where(first_head, sin, pltpu.roll(sin, HEAD_DIM, 1))
    tab_a = _rope_tables(cos_a, sin_a, 0, A_ROT_DIM // 2, HEAD_DIM)
    tab_m = _rope_tables(cos, sin, MLA_NOPE, MLA_ROPE // 2, V7X_LANES)

    proj = _dot(hn, wa_ref[...])
    for c in range(A_WIDTH // V7X_LANES):
        sl = slice(c * V7X_LANES, (c + 1) * V7X_LANES)
        q = proj[:, sl]
        k = proj[:, A_WIDTH + c * V7X_LANES:A_WIDTH + (c + 1) * V7X_LANES]
        qa_ref[c] = _apply_rope(q, tab_a, A_ROT_DIM // 2) * a_scale
        ka_ref[c] = _apply_rope(k, tab_a, A_ROT_DIM // 2)
        va_ref[c] = proj[:, 2 * A_WIDTH + c * V7X_LANES:2 * A_WIDTH + (c + 1) * V7X_LANES]

    cq = _rms(_dot(hn, wcq_ref[...]), gcq_ref[...]).astype(BF16)
    qm = _dot(cq, wuq_ref[...])
    ckv = _rms(_dot(hn, wckv_ref[...]), gckv_ref[...]).astype(BF16)
    kn = _dot(ckv, wuk_ref[...])
    kr = _apply_rope(_dot(hn, wkr_ref[...]), tab_m, MLA_ROPE // 2)
    for h in range(B_HEADS):
        sl = slice(h * V7X_LANES, (h + 1) * V7X_LANES)
        qm_ref[:, sl] = (_apply_rope(qm[:, sl], tab_m, MLA_ROPE // 2) * m_scale).astype(BF16)
        km_ref[:, sl] = (kn[:, sl] + kr).astype(BF16)
    vmt_ref[...] = _dot_nt(wuvt_ref[...], ckv).astype(BF16).reshape(vmt_ref.shape)


def _mix_in(x2, pos2, g_mix, w_in, g_cq, w_uq, g_ckv, w_ukv, tm):
    T, D = x2.shape
    nope_rope = MLA_NOPE + MLA_ROPE
    wa = w_in[:, :3 * A_WIDTH].astype(BF16)
    o0 = 3 * A_WIDTH
    wcq = w_in[:, o0:o0 + MLA_Q_RANK].astype(BF16)
    wckv = w_in[:, o0 + MLA_Q_RANK:o0 + MLA_Q_RANK + MLA_KV_RANK].astype(BF16)
    wkr_raw = w_in[:, o0 + MLA_Q_RANK + MLA_KV_RANK:]
    wkr = jnp.pad(wkr_raw, ((0, 0), (MLA_NOPE, V7X_LANES - nope_rope))).astype(BF16)
    wuq = jnp.pad(w_uq.reshape(MLA_Q_RANK, B_HEADS, nope_rope),
                  ((0, 0), (0, 0), (0, V7X_LANES - nope_rope))
                  ).reshape(MLA_Q_RANK, B_HEADS * V7X_LANES).astype(BF16)
    wukv = w_ukv.reshape(MLA_KV_RANK, B_HEADS, MLA_NOPE + MLA_V)
    wuk = jnp.pad(wukv[:, :, :MLA_NOPE], ((0, 0), (0, 0), (0, V7X_LANES - MLA_NOPE))
                  ).reshape(MLA_KV_RANK, B_HEADS * V7X_LANES).astype(BF16)
    wuvt = wukv[:, :, MLA_NOPE:].reshape(MLA_KV_RANK, B_WIDTH).T.astype(BF16)
    lane = jnp.arange(V7X_LANES)
    ha, hm = A_ROT_DIM // 2, MLA_ROPE // 2
    inv_a = (ROPE_THETA ** (-jnp.arange(ha, dtype=F32) / ha))[lane % ha]
    inv_m = (ROPE_THETA ** (-jnp.arange(hm, dtype=F32) / hm))[lane % hm]
    inv = jnp.where(lane < A_ROT_DIM, inv_a,
                    jnp.where((lane >= MLA_NOPE) & (lane < nope_rope), inv_m, 0.0)
                    ).reshape(1, V7X_LANES)

    full = lambda a: pl.BlockSpec(a.shape, lambda i: (0,) * a.ndim)
    row = lambda w: pl.BlockSpec((tm, w), lambda i: (i, 0))
    consts = (g_mix.reshape(1, D), wa, wcq, wckv, wkr, g_cq.reshape(1, -1), wuq,
              g_ckv.reshape(1, -1), wuk, wuvt, inv)
    a_shape = (A_WIDTH // V7X_LANES, tm, V7X_LANES)
    m_width = B_HEADS * V7X_LANES
    vt_shape = (B_HEADS // 2, 2 * MLA_V, tm)
    blocks = (_nbytes((tm, D), F32) + sum(_nbytes(c.shape, c.dtype) for c in consts)
              + 3 * _nbytes(a_shape, F32) + 2 * _nbytes((tm, m_width), BF16)
              + _nbytes(vt_shape, BF16))
    temps = _nbytes((tm, 3 * A_WIDTH + 4 * B_HEADS * V7X_LANES), F32)
    a_spec = pl.BlockSpec(a_shape, lambda i: (0, i, 0))
    a_out = jax.ShapeDtypeStruct((a_shape[0], T, V7X_LANES), F32)
    return pl.pallas_call(
        functools.partial(_mix_in_kernel, a_scale=HEAD_DIM ** -0.5,
                          m_scale=nope_rope ** -0.5 * math.log2(math.e)),
        grid=(T // tm,),
        in_specs=[row(D), pl.BlockSpec((tm, 1), lambda i: (i, 0))] + [full(c) for c in consts],
        out_specs=[a_spec] * 3 + [row(m_width)] * 2
        + [pl.BlockSpec((None,) + vt_shape, lambda i: (i, 0, 0, 0))],
        out_shape=[a_out] * 3 + [jax.ShapeDtypeStruct((T, m_width), BF16)] * 2
        + [jax.ShapeDtypeStruct((T // tm,) + vt_shape, BF16)],
        compiler_params=pltpu.CompilerParams(
            dimension_semantics=("parallel",), vmem_limit_bytes=_vmem_limit(blocks, temps)),
        name="mix_in",
    )(x2, pos2, *consts)


def _dilated_kernel(q_ref, kp_ref, kc_ref, vp_ref, vc_ref, o_ref, *scratch, patterns, rows):
    n = pl.program_id(2)
    o_scr, l_scr = scratch[:len(patterns)], scratch[len(patterns):]
    lane = lax.broadcasted_iota(I32, (1, V7X_LANES), 1)
    lo = lane < HEAD_DIM
    qi = lax.broadcasted_iota(I32, (BLK, 2 * BLK), 0)
    kj = lax.broadcasted_iota(I32, (BLK, 2 * BLK), 1)
    dist = qi + BLK - kj

    for pi, (span, dil) in enumerate(patterns):
        band = (dist >= 0) & (dist <= span)
        first_valid = band & (kj >= jnp.where(n > 0, 0, BLK))
        step = BLK * dil

        def take(ref, start, dil=dil):
            idx = pl.ds(start, BLK, stride=dil) if dil > 1 else pl.ds(start, BLK)
            return ref[idx, :]

        def block(r, j, pi=pi, dil=dil, step=step, take=take, band=band, first_valid=first_valid):
            start = j * step + r
            q2 = take(q_ref, start).astype(BF16)
            if j == 0:
                kprev, vprev = take(kp_ref, rows - step + r), take(vp_ref, rows - step + r)
                valid = first_valid
            else:
                kprev, vprev = take(kc_ref, start - step), take(vc_ref, start - step)
                valid = band
            k2 = jnp.concatenate([kprev, take(kc_ref, start)], axis=0).astype(BF16)
            v2 = jnp.concatenate([vprev, take(vc_ref, start)], axis=0).astype(BF16)
            acc = jnp.zeros((BLK, V7X_LANES), F32)
            lse2 = jnp.zeros((BLK, V7X_LANES), F32)
            for h in range(2):
                sel = lo if h == 0 else ~lo
                qh = jnp.where(sel, q2, jnp.zeros_like(q2))
                vh = jnp.where(sel, v2, jnp.zeros_like(v2))
                s = jnp.where(valid, _dot_nt(qh, k2), NEG_INF)
                m = jnp.max(s, axis=-1, keepdims=True)
                p = jnp.exp(s - m)
                den = jnp.sum(p, axis=-1, keepdims=True)
                acc = acc + _dot(p.astype(BF16), vh) / den
                lse2 = jnp.where(sel, m + jnp.log(den), lse2)
            idx = pl.ds(start, BLK, stride=dil) if dil > 1 else pl.ds(start, BLK)
            o_scr[pi][idx, :] = acc
            l_scr[pi][idx, :] = lse2

        for r in range(dil):
            for j in range(rows // step):
                block(r, j)

    lses = [l[...] for l in l_scr]
    mx = functools.reduce(jnp.maximum, lses)
    es = [jnp.exp(l - mx) for l in lses]
    num = sum(e * o[...] for e, o in zip(es, o_scr))
    o_ref[...] = (num / sum(es)).astype(o_ref.dtype)


def _dilated(qa, ka, va, B, S):
    patterns = tuple((w // d, d) for w, d in DIL_PATTERNS)
    rows = BLK * max(d for _, d in patterns)
    assert all(span <= BLK for span, _ in patterns) and S % rows == 0
    nb = S // rows
    npairs = qa.shape[0]
    cur = pl.BlockSpec((None, rows, V7X_LANES), lambda b, h, n: (h, b * nb + n, 0))
    prev = pl.BlockSpec((None, rows, V7X_LANES),
                        lambda b, h, n: (h, b * nb + jnp.maximum(n - 1, 0), 0))
    blocks = 5 * _nbytes((rows, V7X_LANES), F32) + _nbytes((rows, V7X_LANES), BF16)
    scratch = 2 * len(patterns) * _nbytes((rows, V7X_LANES), F32)
    return pl.pallas_call(
        functools.partial(_dilated_kernel, patterns=patterns, rows=rows),
        grid=(B, npairs, nb),
        in_specs=[cur, prev, cur, prev, cur],
        out_specs=cur,
        out_shape=jax.ShapeDtypeStruct(qa.shape, BF16),
        scratch_shapes=[pltpu.VMEM((rows, V7X_LANES), F32)] * (2 * len(patterns)),
        compiler_params=pltpu.CompilerParams(
            dimension_semantics=("parallel", "parallel", "arbitrary"),
            vmem_limit_bytes=_vmem_limit(blocks, scratch + 4 * 1024 * 1024)),
        name="dilated_attention",
    )(qa, ka, ka, va, va)


def _mla_kernel(q_ref, k_ref, vt_ref, o_ref, s_ref, *, tq, tk):
    qi = pl.program_id(2)
    qs = [q_ref[:, h * V7X_LANES:(h + 1) * V7X_LANES] for h in range(2)]
    q_start = qi * tq

    def scores_into(slot, ki):
        kb = k_ref[pl.ds(pl.multiple_of(ki * tk, tk), tk), :]
        for h in range(2):
            s_ref[slot, h] = _dot_nt(kb[:, h * V7X_LANES:(h + 1) * V7X_LANES], qs[h])

    def consume(slot, ki, carry, masked, refill=None):
        ms, ls, accs = carry
        vtb = vt_ref[ki]
        new_ms, new_ls, alphas, ps = [], [], [], []
        for h in range(2):
            s = s_ref[slot, h]
            if masked:
                kpos = ki * tk + lax.broadcasted_iota(I32, (tk, tq), 0)
                qpos = q_start + lax.broadcasted_iota(I32, (tk, tq), 1)
                s = jnp.where(kpos <= qpos, s, NEG_INF)
            m_new = jnp.maximum(ms[h], jnp.max(s, axis=0, keepdims=True))
            alphas.append(jnp.exp2(ms[h] - m_new))
            p = jnp.exp2(s - m_new)
            new_ls.append(alphas[h] * ls[h] + jnp.sum(p, axis=0, keepdims=True))
            new_ms.append(m_new)
            ps.append(p.astype(BF16))
        if refill is not None:
            scores_into(slot, refill)
        new_accs = [accs[h] * alphas[h] + _dot(vtb[h * MLA_V:(h + 1) * MLA_V, :], ps[h])
                    for h in range(2)]
        return tuple(new_ms), tuple(new_ls), tuple(new_accs)

    def pair(j, carry):
        scores_into(1, 2 * j + 1)
        carry = consume(0, 2 * j, carry, False, refill=2 * j + 2)
        return consume(1, 2 * j + 1, carry, False)

    def even_tail(carry):
        return consume(0, n_full, carry, True)

    def odd_tail(carry):
        scores_into(1, n_full)
        carry = consume(0, n_full - 1, carry, False)
        return consume(1, n_full, carry, True)

    init = ((jnp.full((1, tq), NEG_INF, F32),) * 2, (jnp.zeros((1, tq), F32),) * 2,
            (jnp.zeros((MLA_V, tq), F32),) * 2)
    n_full = q_start // tk
    scores_into(0, 0)
    carry = lax.fori_loop(0, n_full // 2, pair, init)
    if tq == tk:
        _, ls, accs = lax.cond(n_full % 2 == 0, even_tail, odd_tail, carry)
    else:
        scores_into(1, n_full + 1)
        carry = consume(0, n_full, carry, True)
        _, ls, accs = consume(1, n_full + 1, carry, True)
    for h in range(2):
        o_ref[h * MLA_V:(h + 1) * MLA_V, :] = (accs[h] / ls[h]).astype(o_ref.dtype)


def _mla(qm, km, vmt, B, S):
    tk = vmt.shape[-1]
    tq = min(MLA_TQ, S)
    assert tq in (tk, 2 * tk) and S % tq == 0
    nk = S // tk
    hw = 2 * V7X_LANES
    q3, k3 = (a.reshape(B, S, -1) for a in (qm, km))
    blocks = _nbytes((tq, hw), BF16) + _nbytes((S, hw), BF16) + _nbytes((S, V7X_LANES), BF16) \
        + _nbytes((tq, V7X_LANES), BF16)
    score_scratch = (2, 2, tk, tq)
    temps = _nbytes(score_scratch, F32) + 6 * _nbytes((tk, tq), F32)
    return pl.pallas_call(
        functools.partial(_mla_kernel, tq=tq, tk=tk),
        grid=(B, B_HEADS // 2, S // tq),
        in_specs=[pl.BlockSpec((None, tq, hw), lambda b, h, i: (b, i, h)),
                  pl.BlockSpec((None, S, hw), lambda b, h, i: (b, 0, h)),
                  pl.BlockSpec((nk, None, V7X_LANES, tk), lambda b, h, i: (b, h, 0, 0))],
        out_specs=pl.BlockSpec((None, V7X_LANES, tq), lambda b, h, i: (b, h, i)),
        out_shape=jax.ShapeDtypeStruct((B, B_WIDTH, S), BF16),
        scratch_shapes=[pltpu.VMEM(score_scratch, F32)],
        compiler_params=pltpu.CompilerParams(
            dimension_semantics=("parallel", "parallel", "arbitrary"),
            vmem_limit_bytes=_vmem_limit(blocks, temps)),
        name="mla_attention",
    )(q3, k3, vmt)


def _mix_out_kernel(oa_ref, om_ref, x_ref, wo_ref, gffn_ref, wpq_ref, keys_ref,
                    h_ref, hn_ref, i1_ref, i2_ref, gate_ref):
    oa = jnp.concatenate([oa_ref[c] for c in range(oa_ref.shape[0])], axis=1)
    mixed_m = lax.dot_general(om_ref[...], wo_ref[A_WIDTH:, :], (((0,), (0,)), ((), ())),
                              preferred_element_type=F32)
    h = x_ref[...] + _dot(oa, wo_ref[:A_WIDTH, :]) + mixed_m
    h_ref[...] = h
    hn = _rms(h, gffn_ref[...]).astype(BF16)
    hn_ref[...] = hn
    i1, i2, gate = _peer_topk(
        lambda head: _dot(hn, wpq_ref[:, head * PEER_QDIM:(head + 1) * PEER_QDIM]), keys_ref)
    i1_ref[...], i2_ref[...], gate_ref[...] = i1.T, i2.T, gate.T


def _mix_out(oa4, omt, x2, w_out, g_ffn, w_pq, sub_keys):
    T, D = x2.shape
    S = omt.shape[-1]
    tm = min(512, S)
    nt = S // tm
    wo = w_out.astype(BF16)
    wpq = w_pq.astype(BF16)
    keys = sub_keys.reshape(PEER_HEADS * 2, PEER_NKEYS, PEER_QDIM // 2).astype(BF16)
    QW = wpq.shape[1]
    row = lambda w: pl.BlockSpec((tm, w), lambda i: (i, 0))
    full = lambda a: pl.BlockSpec(a.shape, lambda i: (0,) * a.ndim)
    blocks = _nbytes((tm, A_WIDTH), BF16) + _nbytes((tm, B_WIDTH), BF16) \
        + 2 * _nbytes((tm, D), F32) + _nbytes(wo.shape, BF16) + _nbytes(wpq.shape, BF16) \
        + _nbytes(keys.shape, BF16) + _nbytes((tm, D), BF16) + 3 * _nbytes((PEER_SEL, tm), F32)
    temps = 5 * _nbytes((tm, QW), F32) + 16 * _nbytes((2 * PEER_SEL, tm), F32)
    return pl.pallas_call(
        _mix_out_kernel,
        grid=(T // tm,),
        in_specs=[pl.BlockSpec((oa4.shape[0], tm, V7X_LANES), lambda i: (0, i, 0)),
                  pl.BlockSpec((None, B_WIDTH, tm), lambda i: (i // nt, 0, i % nt)), row(D), full(wo),
                  pl.BlockSpec((1, D), lambda i: (0, 0)), full(wpq), full(keys)],
        out_specs=[row(D), row(D), row(PEER_SEL), row(PEER_SEL), row(PEER_SEL)],
        out_shape=[jax.ShapeDtypeStruct((T, D), F32), jax.ShapeDtypeStruct((T, D), BF16),
                   jax.ShapeDtypeStruct((T, PEER_SEL), I32), jax.ShapeDtypeStruct((T, PEER_SEL), I32),
                   jax.ShapeDtypeStruct((T, PEER_SEL), F32)],
        compiler_params=pltpu.CompilerParams(
            dimension_semantics=("parallel",), vmem_limit_bytes=_vmem_limit(blocks, temps)),
        name="mix_out_topk",
    )(oa4, omt, x2, wo, g_ffn.reshape(1, D), wpq, keys)


def _topk_rows(sc, k, order=None, payload=None):
    if order is None:
        order = lax.broadcasted_iota(I32, sc.shape, 0).astype(F32)
    vals, sel = [], []
    for i in range(k):
        m = jnp.max(sc, axis=0, keepdims=True)
        first = jnp.min(jnp.where(sc == m, order, _ORDER_NONE), axis=0, keepdims=True)
        hit = order == first
        if payload is None:
            got = first
        else:
            got = jnp.max(jnp.where(hit, payload, -1.0), axis=0, keepdims=True)
        vals.append(m)
        sel.append(got)
        sc = jnp.where(hit, -jnp.inf, sc)
    return jnp.concatenate(vals, axis=0), jnp.concatenate(sel, axis=0)


_ORDER_NONE = float(1 << 20)


def _candidate_pieces(k):
    pieces = []
    split = 0
    while (split + 1) * (split + 1) <= k:
        split += 1
    for r in range(split):
        n = k // (r + 1)
        for s0 in range(0, n, V7X_SUBLANES):
            pieces.append((True, r, s0, 0, min(V7X_SUBLANES, n - s0)))
    for s in range(split):
        n = k // (s + 1)
        for r0 in range(0, n, V7X_SUBLANES):
            lo, hi = max(split - r0, 0), min(V7X_SUBLANES, n - r0)
            if hi > lo:
                pieces.append((False, s, r0, lo, hi))
    return pieces


def _peer_topk(project, keys_ref):
    K = PEER_TOPK
    pieces = _candidate_pieces(K)
    q_next = project(0)
    tt = q_next.shape[0]
    j8 = lax.broadcasted_iota(I32, (V7X_SUBLANES, tt), 0)
    j8f = j8.astype(F32)
    i1s, i2s, gates = [], [], []
    for h in range(PEER_HEADS):
        q_head = q_next
        if h + 1 < PEER_HEADS:
            q_next = project(h + 1)
        vals, idxs = [], []
        for p in range(2):
            q = q_head[:, p * (PEER_QDIM // 2):(p + 1) * (PEER_QDIM // 2)].astype(BF16)
            sc = _dot_nt(keys_ref[2 * h + p], q)
            v, i = _topk_rows(sc, K)
            vals.append(v)
            idxs.append(i)
        cand, order, pair = [], [], []
        for n, (along_s, fixed, start, lo, hi) in enumerate(pieces):
            a, b = (0, 1) if along_s else (1, 0)
            sl = slice(start, start + V7X_SUBLANES)
            valid = (j8 >= lo) & (j8 < hi)
            cand.append(jnp.where(valid, vals[a][fixed:fixed + 1, :] + vals[b][sl, :], -jnp.inf))
            flat = (fixed * K + start + j8f) if along_s else ((start + j8f) * K + fixed)
            order.append(jnp.where(valid, flat, _ORDER_NONE - 1 - n * V7X_SUBLANES - j8f))
            e1 = idxs[0][fixed:fixed + 1, :] if along_s else idxs[0][sl, :]
            e2 = idxs[1][sl, :] if along_s else idxs[1][fixed:fixed + 1, :]
            pair.append(e1 * PEER_NKEYS + e2)
        best, expert = _topk_rows(jnp.concatenate(cand, axis=0), K,
                                  order=jnp.concatenate(order, axis=0),
                                  payload=jnp.concatenate(pair, axis=0))
        e = jnp.exp(best - jnp.max(best, axis=0, keepdims=True))
        gates.append(e / jnp.sum(e, axis=0, keepdims=True))
        expert = expert.astype(I32)
        i1s.append(expert // PEER_NKEYS)
        i2s.append(expert % PEER_NKEYS)
    return tuple(jnp.concatenate(parts, axis=0) for parts in (i1s, i2s, gates))


def _peer_act_kernel(hn_ref, u_ref, i1_ref, i2_ref, gate_ref, coef_ref, z_ref, *, chunks):
    g = pl.program_id(1)

    @pl.when(g == 0)
    def _():
        z_ref[...] = jnp.zeros_like(z_ref)

    hn = hn_ref[...]
    i1 = i1_ref[...]
    i2 = i2_ref[...]
    zsel = z_ref[...]
    for k in range(chunks // 2):
        z = _dot_nt(hn, u_ref[2 * k * PEER_NKEYS:(2 * k + 2) * PEER_NKEYS, :])
        for j in range(2):
            picked = jnp.take_along_axis(z[:, j * PEER_NKEYS:(j + 1) * PEER_NKEYS], i2, axis=1)
            zsel = jnp.where(i1 == g * chunks + 2 * k + j, picked, zsel)
    z_ref[...] = zsel

    @pl.when(g == pl.num_programs(1) - 1)
    def _():
        act = 0.5 * zsel * (1.0 + lax.erf(zsel * np.float32(math.sqrt(0.5))))
        coef_ref[...] = act * gate_ref[...]


def _peer_act(hn, u_bf, i1, i2, gate):
    T, D = hn.shape
    E = u_bf.shape[0]
    tm = min(1024, T)
    chunks = 32
    rows = chunks * PEER_NKEYS
    sel = pl.BlockSpec((tm, PEER_SEL), lambda i, g: (i, 0))
    blocks = _nbytes((tm, D), BF16) + _nbytes((rows, D), BF16) + 4 * _nbytes((tm, PEER_SEL), F32)
    return pl.pallas_call(
        functools.partial(_peer_act_kernel, chunks=chunks),
        grid=(T // tm, E // rows),
        in_specs=[pl.BlockSpec((tm, D), lambda i, g: (i, 0)),
                  pl.BlockSpec((rows, D), lambda i, g: (g, 0)), sel, sel, sel],
        out_specs=sel,
        out_shape=jax.ShapeDtypeStruct((T, PEER_SEL), F32),
        scratch_shapes=[pltpu.VMEM((tm, PEER_SEL), F32)],
        compiler_params=pltpu.CompilerParams(
            dimension_semantics=("parallel", "arbitrary"),
            vmem_limit_bytes=_vmem_limit(
                blocks, (chunks // 2 + 4) * _nbytes((tm, 2 * PEER_NKEYS), F32))),
        name="peer_act",
    )(hn, u_bf, i1, i2, gate)


def _peer_out_kernel(i1_ref, i2_ref, coef_ref, v_ref, h_ref, gfin_ref, out_ref, sx_ref,
                     *, tt, pitch):
    NK = PEER_NKEYS
    sub = lax.broadcasted_iota(I32, (NK, PEER_SEL), 0)

    for t in range(tt):
        i1r = i1_ref[t:t + 1, :]
        i2r = i2_ref[t:t + 1, :]
        cr = coef_ref[t:t + 1, :]
        a_t = jnp.where(sub == i1r, 1.0, 0.0).astype(BF16)
        b_t = jnp.where(sub == i2r, cr, 0.0).astype(BF16)
        sx_ref[t * pitch:t * pitch + NK, :] = _dot_nt(a_t, b_t)

    acc = jnp.zeros(out_ref.shape, F32)
    for c0 in range(0, v_ref.shape[0] // NK, 2):
        lhs = jnp.concatenate(
            [sx_ref[pl.ds(c0 + k, tt, stride=pitch), :] for k in range(2)],
            axis=1).astype(BF16)
        acc = acc + _dot(lhs, v_ref[c0 * NK:(c0 + 2) * NK, :])
    out_ref[...] = _rms(h_ref[...] + acc, gfin_ref[...])


def _peer_out(i1, i2, coef, v_bf, h, g_final):
    T, D = h.shape
    E = v_bf.shape[0]
    tt = min(256, T)
    assert (E // PEER_NKEYS) % 2 == 0
    pitch = PEER_NKEYS + V7X_SUBLANES
    sel = pl.BlockSpec((tt, PEER_SEL), lambda i: (i, 0))
    tok = pl.BlockSpec((tt, D), lambda i: (i, 0))
    blocks = 3 * _nbytes((tt, PEER_SEL), F32) + 2 * _nbytes((tt, D), F32)
    scratch = _nbytes((tt * pitch, PEER_NKEYS), F32)
    resident = _nbytes((E, D), BF16)
    return pl.pallas_call(
        functools.partial(_peer_out_kernel, tt=tt, pitch=pitch),
        grid=(T // tt,),
        in_specs=[sel, sel, sel,
                  pl.BlockSpec((E, D), lambda i: (0, 0), pipeline_mode=pl.Buffered(1)), tok,
                  pl.BlockSpec((1, D), lambda i: (0, 0))],
        out_specs=tok,
        out_shape=jax.ShapeDtypeStruct((T, D), F32),
        scratch_shapes=[pltpu.VMEM((tt * pitch, PEER_NKEYS), F32)],
        compiler_params=pltpu.CompilerParams(
            dimension_semantics=("parallel",),
            vmem_limit_bytes=_vmem_limit(blocks, resident + scratch + 3 * _nbytes((tt, D), F32))),
        name="peer_out",
    )(i1, i2, coef, v_bf, h, g_final.reshape(1, D))


def kernel(x, positions, g_mix, w_in, g_cq, w_uq, g_ckv, w_ukv, w_out, g_ffn,
           w_pq, sub_keys, u_emb, v_emb, g_final):
    B, S, D = x.shape
    T = B * S
    assert g_mix.shape[0] == 1, "single-layer configuration"
    x2 = x.reshape(T, D)
    pos2 = positions.reshape(T, 1)

    qa, ka, va, qm, km, vmt = _mix_in(x2, pos2, g_mix[0], w_in[0], g_cq[0], w_uq[0],
                                      g_ckv[0], w_ukv[0], tm=min(MLA_TK, S))
    oa4 = _dilated(qa, ka, va, B, S)
    omt = _mla(qm, km, vmt, B, S)
    h, hn, i1, i2, gate = _mix_out(oa4, omt, x2, w_out[0], g_ffn[0], w_pq[0], sub_keys[0])
    coef = _peer_act(hn, u_emb[0].astype(BF16), i1, i2, gate)
    out = _peer_out(i1, i2, coef, v_emb[0].astype(BF16), h, g_final)
    return out.reshape(B, S, D)
```

```python
import functools
import math

import jax
import jax.numpy as jnp
import numpy as np
from jax import lax
from jax.experimental import pallas as pl
from jax.experimental.pallas import tpu as pltpu

F32 = jnp.float32
BF16 = jnp.bfloat16
I32 = jnp.int32

ROPE_THETA = 500000.0
NORM_EPS = 1e-6
NEG_INF = -1e30
BLK = 128
A_HEADS = 8
HEAD_DIM = 64
A_ROT_DIM = HEAD_DIM // 4
DIL_PATTERNS = ((128, 1), (512, 4), (2048, 16))
A_WIDTH = A_HEADS * HEAD_DIM
B_HEADS = 8
MLA_Q_RANK = 256
MLA_KV_RANK = 128
MLA_NOPE = 64
MLA_ROPE = 32
MLA_V = 64
B_WIDTH = B_HEADS * MLA_V
PEER_HEADS = 8
PEER_NKEYS = 128
PEER_QDIM = 256
PEER_TOPK = 16
PEER_SEL = PEER_HEADS * PEER_TOPK

MLA_TQ = 512
MLA_TK = 512

V7X_LANES = 128
V7X_SUBLANES = 8
V7X_VMEM_BYTES = 64 * 1024 * 1024
V7X_VMEM_REQUEST_CAP = 56 * 1024 * 1024


def _vmem_limit(pipelined_bytes, resident_bytes=0):
    est = 2 * int(pipelined_bytes) + int(resident_bytes)
    return int(min(max(est, 16 * 1024 * 1024), V7X_VMEM_REQUEST_CAP))


def _nbytes(shape, dtype):
    return int(np.prod(shape)) * jnp.dtype(dtype).itemsize


def _dot(a, b):
    return jnp.dot(a, b, preferred_element_type=F32)


def _dot_nt(a, b):
    return lax.dot_general(a, b, (((1,), (1,)), ((), ())), preferred_element_type=F32)


def _rms(x, g):
    ms = jnp.mean(x * x, axis=-1, keepdims=True)
    return x * lax.rsqrt(ms + NORM_EPS) * g


def _rope_tables(cos, sin, lo_start, half, period):
    lane = lax.broadcasted_iota(I32, (1, V7X_LANES), 1) & (period - 1)
    is_lo = (lane >= lo_start) & (lane < lo_start + half)
    is_hi = (lane >= lo_start + half) & (lane < lo_start + 2 * half)
    c = jnp.where(is_lo | is_hi, cos, 1.0)
    s_lo = jnp.where(is_lo, -sin, 0.0)
    s_hi = jnp.where(is_hi, sin, 0.0)
    return c, s_lo, s_hi


def _apply_rope(x, tables, half):
    c, s_lo, s_hi = tables
    return (x * c + pltpu.roll(x, V7X_LANES - half, 1) * s_lo
            + pltpu.roll(x, half, 1) * s_hi)


def _mix_in_kernel(x_ref, pos_ref, gmix_ref, wa_ref, wcq_ref, wckv_ref, wkr_ref,
                   gcq_ref, wuq_ref, gckv_ref, wuk_ref, wuvt_ref, inv_ref,
                   qa_ref, ka_ref, va_ref, qm_ref, km_ref, vmt_ref, *, a_scale, m_scale):
    hn = _rms(x_ref[...], gmix_ref[...]).astype(BF16)
    ang = pos_ref[...].astype(F32) * inv_ref[...]
    cos, sin = jnp.cos(ang), jnp.sin(ang)
    first_head = lax.broadcasted_iota(I32, (1, V7X_LANES), 1) < HEAD_DIM
    cos_a = jnp.where(first_head, cos, pltpu.roll(cos, HEAD_DIM, 1))
    sin_a = jnp.where(first_head, sin, pltpu.roll(sin, HEAD_DIM, 1))
    tab_a = _rope_tables(cos_a, sin_a, 0, A_ROT_DIM // 2, HEAD_DIM)
    tab_m = _rope_tables(cos, sin, MLA_NOPE, MLA_ROPE // 2, V7X_LANES)

    proj = _dot(hn, wa_ref[...])
    for c in range(A_WIDTH // V7X_LANES):
        sl = slice(c * V7X_LANES, (c + 1) * V7X_LANES)
        q = proj[:, sl]
        k = proj[:, A_WIDTH + c * V7X_LANES:A_WIDTH + (c + 1) * V7X_LANES]
        qa_ref[c] = _apply_rope(q, tab_a, A_ROT_DIM // 2) * a_scale
        ka_ref[c] = _apply_rope(k, tab_a, A_ROT_DIM // 2)
        va_ref[c] = proj[:, 2 * A_WIDTH + c * V7X_LANES:2 * A_WIDTH + (c + 1) * V7X_LANES]

    cq = _rms(_dot(hn, wcq_ref[...]), gcq_ref[...]).astype(BF16)
    qm = _dot(cq, wuq_ref[...])
    ckv = _rms(_dot(hn, wckv_ref[...]), gckv_ref[...]).astype(BF16)
    kn = _dot(ckv, wuk_ref[...])
    kr = _apply_rope(_dot(hn, wkr_ref[...]), tab_m, MLA_ROPE // 2)
    for h in range(B_HEADS):
        sl = slice(h * V7X_LANES, (h + 1) * V7X_LANES)
        qm_ref[:, sl] = (_apply_rope(qm[:, sl], tab_m, MLA_ROPE // 2) * m_scale).astype(BF16)
        km_ref[:, sl] = (kn[:, sl] + kr).astype(BF16)
    vmt_ref[...] = _dot_nt(wuvt_ref[...], ckv).astype(BF16).reshape(vmt_ref.shape)


def _mix_in(x2, pos2, g_mix, w_in, g_cq, w_uq, g_ckv, w_ukv, tm):
    T, D = x2.shape
    nope_rope = MLA_NOPE + MLA_ROPE
    wa = w_in[:, :3 * A_WIDTH].astype(BF16)
    o0 = 3 * A_WIDTH
    wcq = w_in[:, o0:o0 + MLA_Q_RANK].astype(BF16)
    wckv = w_in[:, o0 + MLA_Q_RANK:o0 + MLA_Q_RANK + MLA_KV_RANK].astype(BF16)
    wkr_raw = w_in[:, o0 + MLA_Q_RANK + MLA_KV_RANK:]
    wkr = jnp.pad(wkr_raw, ((0, 0), (MLA_NOPE, V7X_LANES - nope_rope))).astype(BF16)
    wuq = jnp.pad(w_uq.reshape(MLA_Q_RANK, B_HEADS, nope_rope),
                  ((0, 0), (0, 0), (0, V7X_LANES - nope_rope))
                  ).reshape(MLA_Q_RANK, B_HEADS * V7X_LANES).astype(BF16)
    wukv = w_ukv.reshape(MLA_KV_RANK, B_HEADS, MLA_NOPE + MLA_V)
    wuk = jnp.pad(wukv[:, :, :MLA_NOPE], ((0, 0), (0, 0), (0, V7X_LANES - MLA_NOPE))
                  ).reshape(MLA_KV_RANK, B_HEADS * V7X_LANES).astype(BF16)
    wuvt = wukv[:, :, MLA_NOPE:].reshape(MLA_KV_RANK, B_WIDTH).T.astype(BF16)
    lane = jnp.arange(V7X_LANES)
    ha, hm = A_ROT_DIM // 2, MLA_ROPE // 2
    inv_a = (ROPE_THETA ** (-jnp.arange(ha, dtype=F32) / ha))[lane % ha]
    inv_m = (ROPE_THETA ** (-jnp.arange(hm, dtype=F32) / hm))[lane % hm]
    inv = jnp.where(lane < A_ROT_DIM, inv_a,
                    jnp.where((lane >= MLA_NOPE) & (lane < nope_rope), inv_m, 0.0)
                    ).reshape(1, V7X_LANES)

    full = lambda a: pl.BlockSpec(a.shape, lambda i: (0,) * a.ndim)
    row = lambda w: pl.BlockSpec((tm, w), lambda i: (i, 0))
    consts = (g_mix.reshape(1, D), wa, wcq, wckv, wkr, g_cq.reshape(1, -1), wuq,
              g_ckv.reshape(1, -1), wuk, wuvt, inv)
    a_shape = (A_WIDTH // V7X_LANES, tm, V7X_LANES)
    m_width = B_HEADS * V7X_LANES
    vt_shape = (B_HEADS // 2, 2 * MLA_V, tm)
    blocks = (_nbytes((tm, D), F32) + sum(_nbytes(c.shape, c.dtype) for c in consts)
              + 3 * _nbytes(a_shape, F32) + 2 * _nbytes((tm, m_width), BF16)
              + _nbytes(vt_shape, BF16))
    temps = _nbytes((tm, 3 * A_WIDTH + 4 * B_HEADS * V7X_LANES), F32)
    a_spec = pl.BlockSpec(a_shape, lambda i: (0, i, 0))
    a_out = jax.ShapeDtypeStruct((a_shape[0], T, V7X_LANES), F32)
    return pl.pallas_call(
        functools.partial(_mix_in_kernel, a_scale=HEAD_DIM ** -0.5,
                          m_scale=nope_rope ** -0.5 * math.log2(math.e)),
        grid=(T // tm,),
        in_specs=[row(D), pl.BlockSpec((tm, 1), lambda i: (i, 0))] + [full(c) for c in consts],
        out_specs=[a_spec] * 3 + [row(m_width)] * 2
        + [pl.BlockSpec((None,) + vt_shape, lambda i: (i, 0, 0, 0))],
        out_shape=[a_out] * 3 + [jax.ShapeDtypeStruct((T, m_width), BF16)] * 2
        + [jax.ShapeDtypeStruct((T // tm,) + vt_shape, BF16)],
        compiler_params=pltpu.CompilerParams(
            dimension_semantics=("parallel",), vmem_limit_bytes=_vmem_limit(blocks, temps)),
        name="mix_in",
    )(x2, pos2, *consts)


def _dilated_kernel(q_ref, kp_ref, kc_ref, vp_ref, vc_ref, o_ref, *scratch, patterns, rows):
    n = pl.program_id(2)
    o_scr, l_scr = scratch[:len(patterns)], scratch[len(patterns):]
    lane = lax.broadcasted_iota(I32, (1, V7X_LANES), 1)
    lo = lane < HEAD_DIM
    qi = lax.broadcasted_iota(I32, (BLK, 2 * BLK), 0)
    kj = lax.broadcasted_iota(I32, (BLK, 2 * BLK), 1)
    dist = qi + BLK - kj

    for pi, (span, dil) in enumerate(patterns):
        band = (dist >= 0) & (dist <= span)
        first_valid = band & (kj >= jnp.where(n > 0, 0, BLK))
        step = BLK * dil

        def take(ref, start, dil=dil):
            idx = pl.ds(start, BLK, stride=dil) if dil > 1 else pl.ds(start, BLK)
            return ref[idx, :]

        def block(r, j, pi=pi, dil=dil, step=step, take=take, band=band, first_valid=first_valid):
            start = j * step + r
            q2 = take(q_ref, start).astype(BF16)
            if j == 0:
                kprev, vprev = take(kp_ref, rows - step + r), take(vp_ref, rows - step + r)
                valid = first_valid
            else:
                kprev, vprev = take(kc_ref, start - step), take(vc_ref, start - step)
                valid = band
            k2 = jnp.concatenate([kprev, take(kc_ref, start)], axis=0).astype(BF16)
            v2 = jnp.concatenate([vprev, take(vc_ref, start)], axis=0).astype(BF16)
            acc = jnp.zeros((BLK, V7X_LANES), F32)
            lse2 = jnp.zeros((BLK, V7X_LANES), F32)
            for h in range(2):
                sel = lo if h == 0 else ~lo
                qh = jnp.where(sel, q2, jnp.zeros_like(q2))
                vh = jnp.where(sel, v2, jnp.zeros_like(v2))
                s = jnp.where(valid, _dot_nt(qh, k2), NEG_INF)
                m = jnp.max(s, axis=-1, keepdims=True)
                p = jnp.exp(s - m)
                den = jnp.sum(p, axis=-1, keepdims=True)
                acc = acc + _dot(p.astype(BF16), vh) / den
                lse2 = jnp.where(sel, m + jnp.log(den), lse2)
            idx = pl.ds(start, BLK, stride=dil) if dil > 1 else pl.ds(start, BLK)
            o_scr[pi][idx, :] = acc
            l_scr[pi][idx, :] = lse2

        for r in range(dil):
            for j in range(rows // step):
                block(r, j)

    lses = [l[...] for l in l_scr]
    mx = functools.reduce(jnp.maximum, lses)
    es = [jnp.exp(l - mx) for l in lses]
    num = sum(e * o[...] for e, o in zip(es, o_scr))
    o_ref[...] = (num / sum(es)).astype(o_ref.dtype)


def _dilated(qa, ka, va, B, S):
    patterns = tuple((w // d, d) for w, d in DIL_PATTERNS)
    rows = BLK * max(d for _, d in patterns)
    assert all(span <= BLK for span, _ in patterns) and S % rows == 0
    nb = S // rows
    npairs = qa.shape[0]
    cur = pl.BlockSpec((None, rows, V7X_LANES), lambda b, h, n: (h, b * nb + n, 0))
    prev = pl.BlockSpec((None, rows, V7X_LANES),
                        lambda b, h, n: (h, b * nb + jnp.maximum(n - 1, 0), 0))
    blocks = 5 * _nbytes((rows, V7X_LANES), F32) + _nbytes((rows, V7X_LANES), BF16)
    scratch = 2 * len(patterns) * _nbytes((rows, V7X_LANES), F32)
    return pl.pallas_call(
        functools.partial(_dilated_kernel, patterns=patterns, rows=rows),
        grid=(B, npairs, nb),
        in_specs=[cur, prev, cur, prev, cur],
        out_specs=cur,
        out_shape=jax.ShapeDtypeStruct(qa.shape, BF16),
        scratch_shapes=[pltpu.VMEM((rows, V7X_LANES), F32)] * (2 * len(patterns)),
        compiler_params=pltpu.CompilerParams(
            dimension_semantics=("parallel", "parallel", "arbitrary"),
            vmem_limit_bytes=_vmem_limit(blocks, scratch + 4 * 1024 * 1024)),
        name="dilated_attention",
    )(qa, ka, ka, va, va)


def _mla_kernel(q_ref, k_ref, vt_ref, *rest, tq, tk, n_cast):
    cast_src, o_ref = rest[:n_cast], rest[n_cast]
    cast_dst, s_ref = rest[n_cast + 1:2 * n_cast + 1], rest[2 * n_cast + 1]
    for src, dst in zip(cast_src, cast_dst):
        dst[...] = src[...].astype(BF16)
    qi = pl.program_id(2)
    qs = [q_ref[:, h * V7X_LANES:(h + 1) * V7X_LANES] for h in range(2)]
    q_start = qi * tq

    def scores_into(slot, ki):
        kb = k_ref[pl.ds(pl.multiple_of(ki * tk, tk), tk), :]
        for h in range(2):
            s_ref[slot, h] = _dot_nt(kb[:, h * V7X_LANES:(h + 1) * V7X_LANES], qs[h])

    def consume(slot, ki, carry, masked, refill=None):
        ms, ls, accs = carry
        vtb = vt_ref[ki]
        new_ms, new_ls, alphas, ps = [], [], [], []
        for h in range(2):
            s = s_ref[slot, h]
            if masked:
                kpos = ki * tk + lax.broadcasted_iota(I32, (tk, tq), 0)
                qpos = q_start + lax.broadcasted_iota(I32, (tk, tq), 1)
                s = jnp.where(kpos <= qpos, s, NEG_INF)
            m_new = jnp.maximum(ms[h], jnp.max(s, axis=0, keepdims=True))
            alphas.append(jnp.exp2(ms[h] - m_new))
            p = jnp.exp2(s - m_new)
            new_ls.append(alphas[h] * ls[h] + jnp.sum(p, axis=0, keepdims=True))
            new_ms.append(m_new)
            ps.append(p.astype(BF16))
        if refill is not None:
            scores_into(slot, refill)
        new_accs = [accs[h] * alphas[h] + _dot(vtb[h * MLA_V:(h + 1) * MLA_V, :], ps[h])
                    for h in range(2)]
        return tuple(new_ms), tuple(new_ls), tuple(new_accs)

    def pair(j, carry):
        scores_into(1, 2 * j + 1)
        carry = consume(0, 2 * j, carry, False, refill=2 * j + 2)
        return consume(1, 2 * j + 1, carry, False)

    def even_tail(carry):
        return consume(0, n_full, carry, True)

    def odd_tail(carry):
        scores_into(1, n_full)
        carry = consume(0, n_full - 1, carry, False)
        return consume(1, n_full, carry, True)

    init = ((jnp.full((1, tq), NEG_INF, F32),) * 2, (jnp.zeros((1, tq), F32),) * 2,
            (jnp.zeros((MLA_V, tq), F32),) * 2)
    n_full = q_start // tk
    scores_into(0, 0)
    carry = lax.fori_loop(0, n_full // 2, pair, init)
    if tq == tk:
        _, ls, accs = lax.cond(n_full % 2 == 0, even_tail, odd_tail, carry)
    else:
        scores_into(1, n_full + 1)
        carry = consume(0, n_full, carry, True)
        _, ls, accs = consume(1, n_full + 1, carry, True)
    for h in range(2):
        o_ref[h * MLA_V:(h + 1) * MLA_V, :] = (accs[h] / ls[h]).astype(o_ref.dtype)


def _mla(qm, km, vmt, B, S, tables):
    tk = vmt.shape[-1]
    tq = min(MLA_TQ, S)
    assert tq in (tk, 2 * tk) and S % tq == 0
    nk, nq, nh = S // tk, S // tq, B_HEADS // 2
    hw = 2 * V7X_LANES
    q3, k3 = (a.reshape(B, S, -1) for a in (qm, km))
    E, D = tables[0].shape
    slab = E // (B * nh * nq)
    assert slab * B * nh * nq == E and slab % (2 * V7X_SUBLANES) == 0
    slab_spec = pl.BlockSpec((slab, D), lambda b, h, i: ((b * nh + h) * nq + i, 0))
    blocks = _nbytes((tq, hw), BF16) + _nbytes((S, hw), BF16) + _nbytes((S, V7X_LANES), BF16) \
        + _nbytes((tq, V7X_LANES), BF16) + len(tables) * 3 * _nbytes((slab, D), BF16)
    score_scratch = (2, 2, tk, tq)
    temps = _nbytes(score_scratch, F32) + 6 * _nbytes((tk, tq), F32)
    out = pl.pallas_call(
        functools.partial(_mla_kernel, tq=tq, tk=tk, n_cast=len(tables)),
        grid=(B, nh, nq),
        in_specs=[pl.BlockSpec((None, tq, hw), lambda b, h, i: (b, i, h)),
                  pl.BlockSpec((None, S, hw), lambda b, h, i: (b, 0, h)),
                  pl.BlockSpec((nk, None, V7X_LANES, tk), lambda b, h, i: (b, h, 0, 0))]
        + [slab_spec] * len(tables),
        out_specs=[pl.BlockSpec((None, V7X_LANES, tq), lambda b, h, i: (b, h, i))]
        + [slab_spec] * len(tables),
        out_shape=[jax.ShapeDtypeStruct((B, B_WIDTH, S), BF16)]
        + [jax.ShapeDtypeStruct((E, D), BF16)] * len(tables),
        scratch_shapes=[pltpu.VMEM(score_scratch, F32)],
        compiler_params=pltpu.CompilerParams(
            dimension_semantics=("parallel", "parallel", "arbitrary"),
            vmem_limit_bytes=_vmem_limit(blocks, temps)),
        name="mla_attention",
    )(q3, k3, vmt, *tables)
    return out[0], out[1:]


def _mix_out_kernel(oa_ref, om_ref, x_ref, wo_ref, gffn_ref, wpq_ref, keys_ref,
                    h_ref, hn_ref, i1_ref, i2_ref, gate_ref):
    oa = jnp.concatenate([oa_ref[c] for c in range(oa_ref.shape[0])], axis=1)
    mixed_m = lax.dot_general(om_ref[...], wo_ref[A_WIDTH:, :], (((0,), (0,)), ((), ())),
                              preferred_element_type=F32)
    h = x_ref[...] + _dot(oa, wo_ref[:A_WIDTH, :]) + mixed_m
    h_ref[...] = h
    hn = _rms(h, gffn_ref[...]).astype(BF16)
    hn_ref[...] = hn
    i1, i2, gate = _peer_topk(
        lambda head: _dot(hn, wpq_ref[:, head * PEER_QDIM:(head + 1) * PEER_QDIM]), keys_ref)
    i1_ref[...], i2_ref[...], gate_ref[...] = i1.T, i2.T, gate.T


def _mix_out(oa4, omt, x2, w_out, g_ffn, w_pq, sub_keys):
    T, D = x2.shape
    S = omt.shape[-1]
    tm = min(512, S)
    nt = S // tm
    wo = w_out.astype(BF16)
    wpq = w_pq.astype(BF16)
    keys = sub_keys.reshape(PEER_HEADS * 2, PEER_NKEYS, PEER_QDIM // 2).astype(BF16)
    QW = wpq.shape[1]
    row = lambda w: pl.BlockSpec((tm, w), lambda i: (i, 0))
    full = lambda a: pl.BlockSpec(a.shape, lambda i: (0,) * a.ndim)
    blocks = _nbytes((tm, A_WIDTH), BF16) + _nbytes((tm, B_WIDTH), BF16) \
        + 2 * _nbytes((tm, D), F32) + _nbytes(wo.shape, BF16) + _nbytes(wpq.shape, BF16) \
        + _nbytes(keys.shape, BF16) + _nbytes((tm, D), BF16) + 3 * _nbytes((PEER_SEL, tm), F32)
    temps = 5 * _nbytes((tm, QW), F32) + 16 * _nbytes((2 * PEER_SEL, tm), F32)
    return pl.pallas_call(
        _mix_out_kernel,
        grid=(T // tm,),
        in_specs=[pl.BlockSpec((oa4.shape[0], tm, V7X_LANES), lambda i: (0, i, 0)),
                  pl.BlockSpec((None, B_WIDTH, tm), lambda i: (i // nt, 0, i % nt)), row(D), full(wo),
                  pl.BlockSpec((1, D), lambda i: (0, 0)), full(wpq), full(keys)],
        out_specs=[row(D), row(D), row(PEER_SEL), row(PEER_SEL), row(PEER_SEL)],
        out_shape=[jax.ShapeDtypeStruct((T, D), F32), jax.ShapeDtypeStruct((T, D), BF16),
                   jax.ShapeDtypeStruct((T, PEER_SEL), I32), jax.ShapeDtypeStruct((T, PEER_SEL), I32),
                   jax.ShapeDtypeStruct((T, PEER_SEL), F32)],
        compiler_params=pltpu.CompilerParams(
            dimension_semantics=("parallel",), vmem_limit_bytes=_vmem_limit(blocks, temps)),
        name="mix_out_topk",
    )(oa4, omt, x2, wo, g_ffn.reshape(1, D), wpq, keys)


def _topk_rows(sc, k, order=None, payload=None):
    if order is None:
        order = lax.broadcasted_iota(I32, sc.shape, 0).astype(F32)
    vals, sel = [], []
    for i in range(k):
        m = jnp.max(sc, axis=0, keepdims=True)
        first = jnp.min(jnp.where(sc == m, order, _ORDER_NONE), axis=0, keepdims=True)
        hit = order == first
        if payload is None:
            got = first
        else:
            got = jnp.max(jnp.where(hit, payload, -1.0), axis=0, keepdims=True)
        vals.append(m)
        sel.append(got)
        sc = jnp.where(hit, -jnp.inf, sc)
    return jnp.concatenate(vals, axis=0), jnp.concatenate(sel, axis=0)


_ORDER_NONE = float(1 << 20)


def _candidate_pieces(k):
    pieces = []
    split = 0
    while (split + 1) * (split + 1) <= k:
        split += 1
    for r in range(split):
        n = k // (r + 1)
        for s0 in range(0, n, V7X_SUBLANES):
            pieces.append((True, r, s0, 0, min(V7X_SUBLANES, n - s0)))
    for s in range(split):
        n = k // (s + 1)
        for r0 in range(0, n, V7X_SUBLANES):
            lo, hi = max(split - r0, 0), min(V7X_SUBLANES, n - r0)
            if hi > lo:
                pieces.append((False, s, r0, lo, hi))
    return pieces


def _peer_topk(project, keys_ref):
    K = PEER_TOPK
    pieces = _candidate_pieces(K)
    q_next = project(0)
    tt = q_next.shape[0]
    j8 = lax.broadcasted_iota(I32, (V7X_SUBLANES, tt), 0)
    j8f = j8.astype(F32)
    i1s, i2s, gates = [], [], []
    for h in range(PEER_HEADS):
        q_head = q_next
        if h + 1 < PEER_HEADS:
            q_next = project(h + 1)
        vals, idxs = [], []
        for p in range(2):
            q = q_head[:, p * (PEER_QDIM // 2):(p + 1) * (PEER_QDIM // 2)].astype(BF16)
            sc = _dot_nt(keys_ref[2 * h + p], q)
            v, i = _topk_rows(sc, K)
            vals.append(v)
            idxs.append(i)
        cand, order, pair = [], [], []
        for n, (along_s, fixed, start, lo, hi) in enumerate(pieces):
            a, b = (0, 1) if along_s else (1, 0)
            sl = slice(start, start + V7X_SUBLANES)
            valid = (j8 >= lo) & (j8 < hi)
            cand.append(jnp.where(valid, vals[a][fixed:fixed + 1, :] + vals[b][sl, :], -jnp.inf))
            flat = (fixed * K + start + j8f) if along_s else ((start + j8f) * K + fixed)
            order.append(jnp.where(valid, flat, _ORDER_NONE - 1 - n * V7X_SUBLANES - j8f))
            e1 = idxs[0][fixed:fixed + 1, :] if along_s else idxs[0][sl, :]
            e2 = idxs[1][sl, :] if along_s else idxs[1][fixed:fixed + 1, :]
            pair.append(e1 * PEER_NKEYS + e2)
        best, expert = _topk_rows(jnp.concatenate(cand, axis=0), K,
                                  order=jnp.concatenate(order, axis=0),
                                  payload=jnp.concatenate(pair, axis=0))
        e = jnp.exp(best - jnp.max(best, axis=0, keepdims=True))
        gates.append(e / jnp.sum(e, axis=0, keepdims=True))
        expert = expert.astype(I32)
        i1s.append(expert // PEER_NKEYS)
        i2s.append(expert % PEER_NKEYS)
    return tuple(jnp.concatenate(parts, axis=0) for parts in (i1s, i2s, gates))


def _peer_act_kernel(hn_ref, u_ref, i1_ref, i2_ref, gate_ref, coef_ref, z_ref, *, chunks):
    g = pl.program_id(1)

    @pl.when(g == 0)
    def _():
        z_ref[...] = jnp.zeros_like(z_ref)

    hn = hn_ref[...]
    i1 = i1_ref[...]
    i2 = i2_ref[...]
    zsel = z_ref[...]
    for k in range(chunks // 2):
        z = _dot_nt(hn, u_ref[2 * k * PEER_NKEYS:(2 * k + 2) * PEER_NKEYS, :])
        for j in range(2):
            picked = jnp.take_along_axis(z[:, j * PEER_NKEYS:(j + 1) * PEER_NKEYS], i2, axis=1)
            zsel = jnp.where(i1 == g * chunks + 2 * k + j, picked, zsel)
    z_ref[...] = zsel

    @pl.when(g == pl.num_programs(1) - 1)
    def _():
        act = 0.5 * zsel * (1.0 + lax.erf(zsel * np.float32(math.sqrt(0.5))))
        coef_ref[...] = act * gate_ref[...]


def _peer_act(hn, u_bf, i1, i2, gate):
    T, D = hn.shape
    E = u_bf.shape[0]
    tm = min(1024, T)
    chunks = 32
    rows = chunks * PEER_NKEYS
    sel = pl.BlockSpec((tm, PEER_SEL), lambda i, g: (i, 0))
    blocks = _nbytes((tm, D), BF16) + _nbytes((rows, D), BF16) + 4 * _nbytes((tm, PEER_SEL), F32)
    return pl.pallas_call(
        functools.partial(_peer_act_kernel, chunks=chunks),
        grid=(T // tm, E // rows),
        in_specs=[pl.BlockSpec((tm, D), lambda i, g: (i, 0)),
                  pl.BlockSpec((rows, D), lambda i, g: (g, 0)), sel, sel, sel],
        out_specs=sel,
        out_shape=jax.ShapeDtypeStruct((T, PEER_SEL), F32),
        scratch_shapes=[pltpu.VMEM((tm, PEER_SEL), F32)],
        compiler_params=pltpu.CompilerParams(
            dimension_semantics=("parallel", "arbitrary"),
            vmem_limit_bytes=_vmem_limit(
                blocks, (chunks // 2 + 4) * _nbytes((tm, 2 * PEER_NKEYS), F32))),
        name="peer_act",
    )(hn, u_bf, i1, i2, gate)


def _peer_out_kernel(i1_ref, i2_ref, coef_ref, v_ref, h_ref, gfin_ref, out_ref, sx_ref,
                     *, tt, pitch):
    NK = PEER_NKEYS
    sub = lax.broadcasted_iota(I32, (NK, PEER_SEL), 0)

    for t in range(tt):
        i1r = i1_ref[t:t + 1, :]
        i2r = i2_ref[t:t + 1, :]
        cr = coef_ref[t:t + 1, :]
        a_t = jnp.where(sub == i1r, 1.0, 0.0).astype(BF16)
        b_t = jnp.where(sub == i2r, cr, 0.0).astype(BF16)
        sx_ref[t * pitch:t * pitch + NK, :] = _dot_nt(a_t, b_t)

    acc = jnp.zeros(out_ref.shape, F32)
    for c0 in range(0, v_ref.shape[0] // NK, 2):
        lhs = jnp.concatenate(
            [sx_ref[pl.ds(c0 + k, tt, stride=pitch), :] for k in range(2)],
            axis=1).astype(BF16)
        acc = acc + _dot(lhs, v_ref[c0 * NK:(c0 + 2) * NK, :])
    out_ref[...] = _rms(h_ref[...] + acc, gfin_ref[...])


def _peer_out(i1, i2, coef, v_bf, h, g_final):
    T, D = h.shape
    E = v_bf.shape[0]
    tt = min(256, T)
    assert (E // PEER_NKEYS) % 2 == 0
    pitch = PEER_NKEYS + V7X_SUBLANES
    sel = pl.BlockSpec((tt, PEER_SEL), lambda i: (i, 0))
    tok = pl.BlockSpec((tt, D), lambda i: (i, 0))
    blocks = 3 * _nbytes((tt, PEER_SEL), F32) + 2 * _nbytes((tt, D), F32)
    scratch = _nbytes((tt * pitch, PEER_NKEYS), F32)
    resident = _nbytes((E, D), BF16)
    return pl.pallas_call(
        functools.partial(_peer_out_kernel, tt=tt, pitch=pitch),
        grid=(T // tt,),
        in_specs=[sel, sel, sel,
                  pl.BlockSpec((E, D), lambda i: (0, 0), pipeline_mode=pl.Buffered(1)), tok,
                  pl.BlockSpec((1, D), lambda i: (0, 0))],
        out_specs=tok,
        out_shape=jax.ShapeDtypeStruct((T, D), F32),
        scratch_shapes=[pltpu.VMEM((tt * pitch, PEER_NKEYS), F32)],
        compiler_params=pltpu.CompilerParams(
            dimension_semantics=("parallel",),
            vmem_limit_bytes=_vmem_limit(blocks, resident + scratch + 3 * _nbytes((tt, D), F32))),
        name="peer_out",
    )(i1, i2, coef, v_bf, h, g_final.reshape(1, D))


def kernel(x, positions, g_mix, w_in, g_cq, w_uq, g_ckv, w_ukv, w_out, g_ffn,
           w_pq, sub_keys, u_emb, v_emb, g_final):
    B, S, D = x.shape
    T = B * S
    assert g_mix.shape[0] == 1, "single-layer configuration"
    x2 = x.reshape(T, D)
    pos2 = positions.reshape(T, 1)

    qa, ka, va, qm, km, vmt = _mix_in(x2, pos2, g_mix[0], w_in[0], g_cq[0], w_uq[0],
                                      g_ckv[0], w_ukv[0], tm=min(MLA_TK, S))
    oa4 = _dilated(qa, ka, va, B, S)
    omt, (u_bf, v_bf) = _mla(qm, km, vmt, B, S, (u_emb[0], v_emb[0]))
    h, hn, i1, i2, gate = _mix_out(oa4, omt, x2, w_out[0], g_ffn[0], w_pq[0], sub_keys[0])
    coef = _peer_act(hn, u_bf, i1, i2, gate)
    out = _peer_out(i1, i2, coef, v_bf, h, g_final)
    return out.reshape(B, S, D)
```
